```python
import jax, jax.numpy as jnp
from jax import lax
import numpy as np

D_MODEL = 2048
BATCH = 1
SEQ = 8192
DEPTH = 4

N_A_LAYERS = DEPTH // 2
N_B_LAYERS = DEPTH - N_A_LAYERS
DEEPNORM_ALPHA = (2 * DEPTH) ** 0.25
DEEPNORM_BETA = (8 * DEPTH) ** -0.25
LN_EPS = 1e-5
RMS_EPS = 1e-6
D_RNN = D_MODEL
RG_HEADS = 8
RG_BLOCK = D_RNN // RG_HEADS
CONV_WIDTH = 4
RG_C = 8.0
MLA_HEADS = 16
Q_LORA_RANK = 768
KV_LORA_RANK = 512
QK_NOPE_DIM = 128
QK_ROPE_DIM = 64
V_HEAD_DIM = 128
ROPE_THETA = 10000.0
Q_BLOCK = 128
ATTN_SCALE = (QK_NOPE_DIM + QK_ROPE_DIM) ** -0.5
N_EXPERTS = 32
TOP_K = 4
D_EXPERT = 1024
SWIGLU_LIMIT = 7.0
SWIGLU_ALPHA = 1.702
ROW_BLOCK = 128
PLE_DIM = 256

kernel_name = 'hawk_yoco_mla_moe_deepnorm_ple'


def layer_norm(x, g, b):
    xf = x.astype(jnp.float32)
    mu = jnp.mean(xf, axis=-1, keepdims=True)
    var = jnp.mean(jnp.square(xf - mu), axis=-1, keepdims=True)
    return ((xf - mu) * lax.rsqrt(var + LN_EPS) * g + b).astype(x.dtype)


def rms_norm(x, g):
    xf = x.astype(jnp.float32)
    y = xf * lax.rsqrt(jnp.mean(jnp.square(xf), axis=-1, keepdims=True) + RMS_EPS)
    return (y * g).astype(x.dtype)


def rope_tables(positions):
    inv = 1.0 / (ROPE_THETA ** (jnp.arange(0, QK_ROPE_DIM, 2, dtype=jnp.float32) / QK_ROPE_DIM))
    ang = positions.astype(jnp.float32)[..., None] * inv
    return jnp.cos(ang), jnp.sin(ang)


def apply_rope(t, cos, sin):
    half = QK_ROPE_DIM // 2
    t1, t2 = t[..., :half], t[..., half:]
    c, s = cos.astype(t.dtype), sin.astype(t.dtype)
    return jnp.concatenate([t1 * c - t2 * s, t2 * c + t1 * s], axis=-1)


def causal_conv(x, w, b):
    s_len = x.shape[1]
    xp = jnp.pad(x, ((0, 0), (CONV_WIDTH - 1, 0), (0, 0)))
    out = b
    for k in range(CONV_WIDTH):
        out = out + w[k] * xp[:, k:k + s_len]
    return out


def _linear_recurrence_combine(c1, c2):
    a1, b1 = c1
    a2, b2 = c2
    return a1 * a2, a2 * b1 + b2


def rg_lru(x, gate_a_w, gate_a_b, gate_x_w, gate_x_b, lam):
    bsz, s_len, ch = x.shape
    xb = x.reshape(bsz, s_len, RG_HEADS, RG_BLOCK)
    r = jax.nn.sigmoid((jnp.einsum('bshi,hij->bshj', xb, gate_a_w) + gate_a_b).astype(jnp.float32)).reshape(bsz, s_len, ch)
    ig = jax.nn.sigmoid((jnp.einsum('bshi,hij->bshj', xb, gate_x_w) + gate_x_b).astype(jnp.float32)).reshape(bsz, s_len, ch)
    log_a = RG_C * r * jax.nn.log_sigmoid(lam.astype(jnp.float32))
    a = jnp.exp(log_a)
    b = jnp.sqrt(-jnp.expm1(2.0 * log_a)) * (ig * x.astype(jnp.float32))
    _, h = lax.associative_scan(_linear_recurrence_combine, (a, b), axis=1)
    return h.astype(x.dtype)


def recurrent_block(h, w_in, conv_w, conv_b, gate_a_w, gate_a_b, gate_x_w, gate_x_b, lam, w_out):
    u = h @ w_in
    gate_branch = jax.nn.gelu(u[..., :D_RNN], approximate=True)
    rec = causal_conv(u[..., D_RNN:], conv_w, conv_b)
    rec = rg_lru(rec, gate_a_w, gate_a_b, gate_x_w, gate_x_b, lam)
    return (gate_branch * rec) @ w_out


def mla_shared_kv(h, w_dkv, kv_norm, w_ukv, cos, sin):
    bsz, s_len, _ = h.shape
    ckv = h @ w_dkv
    c = rms_norm(ckv[..., :KV_LORA_RANK], kv_norm)
    k_pe = apply_rope(ckv[..., KV_LORA_RANK:], cos, sin)
    kv = (c @ w_ukv).reshape(bsz, s_len, MLA_HEADS, QK_NOPE_DIM + V_HEAD_DIM)
    return kv[..., :QK_NOPE_DIM], k_pe, kv[..., QK_NOPE_DIM:]


def mla_block(h, w_dq, q_norm, w_uq, w_o, k_nope, k_pe, v, cos, sin):
    bsz, s_len, _ = h.shape
    cq = rms_norm(h @ w_dq, q_norm)
    q = (cq @ w_uq).reshape(bsz, s_len, MLA_HEADS, QK_NOPE_DIM + QK_ROPE_DIM)
    q_nope = q[..., :QK_NOPE_DIM]
    q_pe = apply_rope(q[..., QK_NOPE_DIM:], cos[:, :, None, :], sin[:, :, None, :])
    nqb = s_len // Q_BLOCK
    qn = q_nope.reshape(bsz, nqb, Q_BLOCK, MLA_HEADS, QK_NOPE_DIM).transpose(1, 0, 2, 3, 4)
    qp = q_pe.reshape(bsz, nqb, Q_BLOCK, MLA_HEADS, QK_ROPE_DIM).transpose(1, 0, 2, 3, 4)
    k_idx = jnp.arange(s_len)

    def attend_block(args):
        qn_b, qp_b, blk = args
        s = (jnp.einsum('bqhd,bkhd->bhqk', qn_b, k_nope) + jnp.einsum('bqhd,bkd->bhqk', qp_b, k_pe)).astype(jnp.float32) * ATTN_SCALE
        q_idx = blk * Q_BLOCK + jnp.arange(Q_BLOCK)
        s = jnp.where(k_idx[None, :] <= q_idx[:, None], s, jnp.float32(-1e30))
        pr = jax.nn.softmax(s, axis=-1).astype(v.dtype)
        return jnp.einsum('bhqk,bkhd->bqhd', pr, v)

    o = lax.map(attend_block, (qn, qp, jnp.arange(nqb)))
    o = o.transpose(1, 0, 2, 3, 4).reshape(bsz, s_len, MLA_HEADS * V_HEAD_DIM)
    return o @ w_o


def moe(h, router_w, router_b, w1, b1, w2, b2):
    bsz, s_len, d = h.shape
    n_tok = bsz * s_len
    xf = h.reshape(n_tok, d)
    logits = (xf @ router_w + router_b).astype(jnp.float32)
    top_vals, top_idx = lax.top_k(logits, TOP_K)
    gates = jax.nn.softmax(top_vals, axis=-1)
    n_asg = n_tok * TOP_K
    flat_e = top_idx.reshape(n_asg)
    flat_tok = jnp.repeat(jnp.arange(n_tok, dtype=jnp.int32), TOP_K)
    flat_g = gates.reshape(n_asg)
    order = jnp.argsort(flat_e)
    sorted_e = flat_e[order]
    counts = jnp.bincount(flat_e, length=N_EXPERTS)
    padded = (counts + ROW_BLOCK - 1) // ROW_BLOCK * ROW_BLOCK
    pad_end = jnp.cumsum(padded)
    pad_start = pad_end - padded
    start = jnp.cumsum(counts) - counts
    dest = pad_start[sorted_e] + (jnp.arange(n_asg, dtype=jnp.int32) - start[sorted_e])
    n_blocks = (n_asg + ROW_BLOCK - 1) // ROW_BLOCK + N_EXPERTS
    n_rows = n_blocks * ROW_BLOCK
    buf_tok = jnp.full((n_rows,), n_tok, jnp.int32).at[dest].set(flat_tok[order])
    buf_g = jnp.zeros((n_rows,), jnp.float32).at[dest].set(flat_g[order])
    block_e = jnp.minimum(jnp.searchsorted(pad_end, jnp.arange(n_blocks) * ROW_BLOCK, side='right'), N_EXPERTS - 1)
    x_pad = jnp.concatenate([xf, jnp.zeros((1, d), xf.dtype)], axis=0)
    xb = x_pad[buf_tok].reshape(n_blocks, ROW_BLOCK, d)

    def expert_rows(args):
        x_blk, e = args
        hh = x_blk @ w1[e] + b1[e]
        glu = jnp.minimum(hh[..., :D_EXPERT], SWIGLU_LIMIT)
        lin = jnp.clip(hh[..., D_EXPERT:], -SWIGLU_LIMIT, SWIGLU_LIMIT)
        act = glu * jax.nn.sigmoid(SWIGLU_ALPHA * glu) * (lin + 1.0)
        return act @ w2[e] + b2[e]

    yb = lax.map(expert_rows, (xb, block_e)).reshape(n_rows, d)
    y = jax.ops.segment_sum((yb * buf_g[:, None]).astype(yb.dtype), buf_tok, num_segments=n_tok + 1)[:n_tok]
    return y.reshape(bsz, s_len, d)


def setup_inputs(seed: int = 0) -> dict:
    key = jax.random.key(seed)
    ks = iter(jax.random.split(key, 40))

    def nrm(shape, scale):
        return jax.random.normal(next(ks), shape, jnp.float32) * scale

    def gain(shape):
        return 1.0 + nrm(shape, 0.02)

    x = nrm((BATCH, SEQ, D_MODEL), 1.0)
    p = nrm((DEPTH, BATCH, SEQ, PLE_DIM), 1.0)
    offset = jax.random.randint(next(ks), (BATCH, 1), 0, 1024, jnp.int32)
    positions = (offset + jnp.arange(SEQ, dtype=jnp.int32)[None, :]).astype(jnp.int32)
    u = jax.random.uniform(next(ks), (N_A_LAYERS, D_RNN), jnp.float32, 0.9, 0.999)
    a_base = u ** (1.0 / RG_C)
    rg_lambda = jnp.log(a_base) - jnp.log1p(-a_base)
    return {
        'x': x,
        'p': p,
        'positions': positions,
        'ln_mix_g': gain((DEPTH, D_MODEL)),
        'ln_mix_b': nrm((DEPTH, D_MODEL), 0.01),
        'ln_ffn_g': gain((DEPTH, D_MODEL)),
        'ln_ffn_b': nrm((DEPTH, D_MODEL), 0.01),
        'rg_w_in': nrm((N_A_LAYERS, D_MODEL, 2 * D_RNN), D_MODEL ** -0.5),
        'rg_conv_w': nrm((N_A_LAYERS, CONV_WIDTH, D_RNN), CONV_WIDTH ** -0.5),
        'rg_conv_b': nrm((N_A_LAYERS, D_RNN), 0.01),
        'rg_gate_a_w': nrm((N_A_LAYERS, RG_HEADS, RG_BLOCK, RG_BLOCK), RG_BLOCK ** -0.5),
        'rg_gate_a_b': nrm((N_A_LAYERS, RG_HEADS, RG_BLOCK), 0.01),
        'rg_gate_x_w': nrm((N_A_LAYERS, RG_HEADS, RG_BLOCK, RG_BLOCK), RG_BLOCK ** -0.5),
        'rg_gate_x_b': nrm((N_A_LAYERS, RG_HEADS, RG_BLOCK), 0.01),
        'rg_lambda': rg_lambda,
        'rg_w_out': nrm((N_A_LAYERS, D_RNN, D_MODEL), DEEPNORM_BETA * D_RNN ** -0.5),
        'mla_w_dq': nrm((N_B_LAYERS, D_MODEL, Q_LORA_RANK), D_MODEL ** -0.5),
        'mla_q_norm': gain((N_B_LAYERS, Q_LORA_RANK)),
        'mla_w_uq': nrm((N_B_LAYERS, Q_LORA_RANK, MLA_HEADS * (QK_NOPE_DIM + QK_ROPE_DIM)), Q_LORA_RANK ** -0.5),
        'mla_w_o': nrm((N_B_LAYERS, MLA_HEADS * V_HEAD_DIM, D_MODEL), DEEPNORM_BETA * (MLA_HEADS * V_HEAD_DIM) ** -0.5),
        'kv_w_dkv': nrm((D_MODEL, KV_LORA_RANK + QK_ROPE_DIM), D_MODEL ** -0.5),
        'kv_norm': gain((KV_LORA_RANK,)),
        'kv_w_ukv': nrm((KV_LORA_RANK, MLA_HEADS * (QK_NOPE_DIM + V_HEAD_DIM)), KV_LORA_RANK ** -0.5),
        'moe_router_w': nrm((DEPTH, D_MODEL, N_EXPERTS), D_MODEL ** -0.5),
        'moe_router_b': nrm((DEPTH, N_EXPERTS), 0.01),
        'moe_w1': nrm((DEPTH, N_EXPERTS, D_MODEL, 2 * D_EXPERT), D_MODEL ** -0.5),
        'moe_b1': nrm((DEPTH, N_EXPERTS, 2 * D_EXPERT), 0.01),
        'moe_w2': nrm((DEPTH, N_EXPERTS, D_EXPERT, D_MODEL), DEEPNORM_BETA * D_EXPERT ** -0.5),
        'moe_b2': nrm((DEPTH, N_EXPERTS, D_MODEL), 0.01),
        'ple_w_proj': nrm((DEPTH, PLE_DIM, D_MODEL), PLE_DIM ** -0.5),
        'ple_w_gate': nrm((DEPTH, D_MODEL, D_MODEL), D_MODEL ** -0.5),
    }


def reference(x, p, positions, ln_mix_g, ln_mix_b, ln_ffn_g, ln_ffn_b,
              rg_w_in, rg_conv_w, rg_conv_b, rg_gate_a_w, rg_gate_a_b, rg_gate_x_w, rg_gate_x_b, rg_lambda, rg_w_out,
              mla_w_dq, mla_q_norm, mla_w_uq, mla_w_o, kv_w_dkv, kv_norm, kv_w_ukv,
              moe_router_w, moe_router_b, moe_w1, moe_b1, moe_w2, moe_b2, ple_w_proj, ple_w_gate):
    cos, sin = rope_tables(positions)
    h = x
    shared_kv = None
    for i in range(DEPTH):
        if i < N_A_LAYERS:
            mix = recurrent_block(h, rg_w_in[i], rg_conv_w[i], rg_conv_b[i], rg_gate_a_w[i], rg_gate_a_b[i],
                                  rg_gate_x_w[i], rg_gate_x_b[i], rg_lambda[i], rg_w_out[i])
        else:
            if shared_kv is None:
                shared_kv = mla_shared_kv(h, kv_w_dkv, kv_norm, kv_w_ukv, cos, sin)
            j = i - N_A_LAYERS
            k_nope, k_pe, v = shared_kv
            mix = mla_block(h, mla_w_dq[j], mla_q_norm[j], mla_w_uq[j], mla_w_o[j], k_nope, k_pe, v, cos, sin)
        h = layer_norm(DEEPNORM_ALPHA * h + mix, ln_mix_g[i], ln_mix_b[i])
        ffn = moe(h, moe_router_w[i], moe_router_b[i], moe_w1[i], moe_b1[i], moe_w2[i], moe_b2[i])
        h = layer_norm(DEEPNORM_ALPHA * h + ffn, ln_ffn_g[i], ln_ffn_b[i])
        h = h + (p[i] @ ple_w_proj[i]) * jax.nn.sigmoid(h @ ple_w_gate[i])
    return h
```

```python
import functools

import jax
import jax.numpy as jnp
from jax import lax
from jax.experimental import pallas as pl
from jax.experimental.pallas import tpu as pltpu

F32 = jnp.float32
BF16 = jnp.bfloat16
I32 = jnp.int32
U32 = jnp.uint32

D_MODEL = 2048
DEPTH = 4
N_A_LAYERS = DEPTH // 2
DEEPNORM_ALPHA = (2 * DEPTH) ** 0.25
LN_EPS = 1e-5
RMS_EPS = 1e-6
D_RNN = D_MODEL
RG_HEADS = 8
RG_BLOCK = D_RNN // RG_HEADS
CONV_WIDTH = 4
RG_C = 8.0
MLA_HEADS = 16
Q_LORA_RANK = 768
KV_LORA_RANK = 512
QK_NOPE_DIM = 128
QK_ROPE_DIM = 64
V_HEAD_DIM = 128
ROPE_THETA = 10000.0
ATTN_SCALE = (QK_NOPE_DIM + QK_ROPE_DIM) ** -0.5
N_EXPERTS = 32
TOP_K = 4
D_EXPERT = 1024
SWIGLU_LIMIT = 7.0
SWIGLU_ALPHA = 1.702
PLE_DIM = 256

LANES = 128
SUBLANES = 8
QK_PAD = 2 * LANES
MASK_VALUE = -1e30
NEG_BIG = -3.0e38

VMEM_LIMIT = 56 * 1024 * 1024

NT_DIMS = (((1,), (1,)), ((), ()))


def _params(sem, vmem=VMEM_LIMIT):
    return pltpu.CompilerParams(dimension_semantics=sem, vmem_limit_bytes=vmem)


def _layer_norm(z, g, b):
    mu = jnp.mean(z, axis=-1, keepdims=True)
    zc = z - mu
    var = jnp.mean(zc * zc, axis=-1, keepdims=True)
    return zc * lax.rsqrt(var + LN_EPS) * g + b


def _rms_norm(x, g):
    return x * lax.rsqrt(jnp.mean(x * x, axis=-1, keepdims=True) + RMS_EPS) * g


def _linear_kernel(x_ref, w_ref, o_ref):
    x = x_ref[...].astype(BF16)
    o_ref[...] = jnp.dot(x, w_ref[...], preferred_element_type=F32).astype(o_ref.dtype)


def _linear(x, w, out_dtype, tm, tn):
    m, k = x.shape
    n = w.shape[1]
    return pl.pallas_call(
        _linear_kernel,
        grid=(m // tm, n // tn),
        in_specs=[pl.BlockSpec((tm, k), lambda i, j: (i, 0)),
                  pl.BlockSpec((k, tn), lambda i, j: (0, j))],
        out_specs=pl.BlockSpec((tm, tn), lambda i, j: (i, j)),
        out_shape=jax.ShapeDtypeStruct((m, n), out_dtype),
        compiler_params=_params(("parallel", "parallel")),
        name="linear",
    )(x, w)


def _proj_ln_kernel(x_ref, w_ref, h_ref, g_ref, b_ref, o_ref):
    mix = jnp.dot(x_ref[...].astype(BF16), w_ref[...], preferred_element_type=F32)
    z = DEEPNORM_ALPHA * h_ref[...] + mix
    o_ref[...] = _layer_norm(z, g_ref[...], b_ref[...])


def _proj_ln(x, w, h, g, b, tm):
    m, k = x.shape
    n = w.shape[1]
    return pl.pallas_call(
        _proj_ln_kernel,
        grid=(m // tm,),
        in_specs=[pl.BlockSpec((tm, k), lambda i: (i, 0)),
                  pl.BlockSpec((k, n), lambda i: (0, 0)),
                  pl.BlockSpec((tm, n), lambda i: (i, 0)),
                  pl.BlockSpec((1, n), lambda i: (0, 0)),
                  pl.BlockSpec((1, n), lambda i: (0, 0))],
        out_specs=pl.BlockSpec((tm, n), lambda i: (i, 0)),
        out_shape=jax.ShapeDtypeStruct((m, n), F32),
        compiler_params=_params(("parallel",)),
        name="proj_ln",
    )(x, w, h, g, b)


def _shift_rows(x, d, fill_rows):
    xr = pltpu.roll(x, d, 0)
    row8 = lax.broadcasted_iota(I32, fill_rows.shape, 0)
    top = jnp.where(row8 < d, fill_rows, xr[:SUBLANES])
    return jnp.concatenate([top, xr[SUBLANES:]], axis=0)


def _rglru_kernel(gate_ref, rec_ref, cw_ref, cb_ref, wa_ref, ba_ref, wx_ref, bx_ref, lam_ref,
                  o_ref, tail_ref, state_ref):
    t_blk = rec_ref.shape[0]

    @pl.when(pl.program_id(1) == 0)
    def _():
        tail_ref[...] = jnp.zeros_like(tail_ref)
        state_ref[...] = jnp.zeros_like(state_ref)

    x = rec_ref[...]
    prev = tail_ref[...]
    conv = cb_ref[...] + cw_ref[CONV_WIDTH - 1:CONV_WIDTH, :] * x
    for d in range(1, CONV_WIDTH):
        shifted = _shift_rows(x, d, pltpu.roll(prev, d, 0))
        conv = conv + cw_ref[CONV_WIDTH - 1 - d:CONV_WIDTH - d, :] * shifted
    tail_ref[...] = x[t_blk - SUBLANES:]

    xb = conv.astype(BF16)
    r = jax.nn.sigmoid(jnp.dot(xb, wa_ref[0], preferred_element_type=F32) + ba_ref[0])
    ig = jax.nn.sigmoid(jnp.dot(xb, wx_ref[0], preferred_element_type=F32) + bx_ref[0])
    lam = lam_ref[...]
    log_sig = jnp.minimum(lam, 0.0) - jnp.log1p(jnp.exp(-jnp.abs(lam)))
    log_a = RG_C * r * log_sig
    a = jnp.exp(log_a)
    th = jnp.tanh(log_a)
    b = jnp.sqrt(-2.0 * th / (1.0 - th)) * (ig * conv)

    ones8 = jnp.ones((SUBLANES, x.shape[1]), F32)
    zeros8 = jnp.zeros((SUBLANES, x.shape[1]), F32)
    s = 1
    while s < t_blk:
        if s < SUBLANES:
            a_sh = _shift_rows(a, s, ones8)
            b_sh = _shift_rows(b, s, zeros8)
        else:
            a_sh = jnp.concatenate([jnp.ones((s, x.shape[1]), F32), a[:t_blk - s]], axis=0)
            b_sh = jnp.concatenate([jnp.zeros((s, x.shape[1]), F32), b[:t_blk - s]], axis=0)
        b = a * b_sh + b
        a = a * a_sh
        s *= 2
    h = a * state_ref[SUBLANES - 1:SUBLANES, :] + b
    state_ref[...] = h[t_blk - SUBLANES:]

    o_ref[...] = (jax.nn.gelu(gate_ref[...], approximate=True) * h).astype(o_ref.dtype)


def _rglru(u, conv_w, conv_b, wa, ba, wx, bx, lam, t_blk):
    s_len = u.shape[0]
    c = RG_BLOCK
    return pl.pallas_call(
        _rglru_kernel,
        grid=(RG_HEADS, s_len // t_blk),
        in_specs=[pl.BlockSpec((t_blk, c), lambda h, t: (t, h)),
                  pl.BlockSpec((t_blk, c), lambda h, t: (t, RG_HEADS + h)),
                  pl.BlockSpec((CONV_WIDTH, c), lambda h, t: (0, h)),
                  pl.BlockSpec((1, c), lambda h, t: (0, h)),
                  pl.BlockSpec((1, c, c), lambda h, t: (h, 0, 0)),
                  pl.BlockSpec((1, 1, c), lambda h, t: (h, 0, 0)),
                  pl.BlockSpec((1, c, c), lambda h, t: (h, 0, 0)),
                  pl.BlockSpec((1, 1, c), lambda h, t: (h, 0, 0)),
                  pl.BlockSpec((1, c), lambda h, t: (0, h))],
        out_specs=pl.BlockSpec((t_blk, c), lambda h, t: (t, h)),
        out_shape=jax.ShapeDtypeStruct((s_len, D_RNN), BF16),
        scratch_shapes=[pltpu.VMEM((SUBLANES, c), F32), pltpu.VMEM((SUBLANES, c), F32)],
        compiler_params=_params(("parallel", "arbitrary")),
        name="rglru",
    )(u, u, conv_w, conv_b, wa, ba, wx, bx, lam)


def _q_proj_kernel(h_ref, wdq_ref, qn_ref, wa_ref, wb_ref, cos_ref, sin_ref, q_ref):
    cq = jnp.dot(h_ref[...].astype(BF16), wdq_ref[...], preferred_element_type=F32)
    cq = _rms_norm(cq, qn_ref[...]).astype(BF16)
    qa = jnp.dot(cq, wa_ref[...], preferred_element_type=F32)
    qb = jnp.dot(cq, wb_ref[...], preferred_element_type=F32)
    cosp = cos_ref[...]
    sinp = sin_ref[...]
    for hd in range(MLA_HEADS):
        nope = qa[:, hd * QK_PAD:hd * QK_PAD + LANES]
        rope = qa[:, hd * QK_PAD + LANES:(hd + 1) * QK_PAD] * cosp + qb[:, hd * LANES:(hd + 1) * LANES] * sinp
        q_ref[hd, :, 0:LANES] = (nope * ATTN_SCALE).astype(BF16)
        q_ref[hd, :, LANES:QK_PAD] = (rope * ATTN_SCALE).astype(BF16)


def _q_proj(h, wdq, qn, wa, wb, cosp, sinp, tm):
    s_len = h.shape[0]
    full = lambda i: (0, 0)
    return pl.pallas_call(
        _q_proj_kernel,
        grid=(s_len // tm,),
        in_specs=[pl.BlockSpec((tm, D_MODEL), lambda i: (i, 0)),
                  pl.BlockSpec(wdq.shape, full),
                  pl.BlockSpec(qn.shape, full),
                  pl.BlockSpec(wa.shape, full),
                  pl.BlockSpec(wb.shape, full),
                  pl.BlockSpec((tm, LANES), lambda i: (i, 0)),
                  pl.BlockSpec((tm, LANES), lambda i: (i, 0))],
        out_specs=pl.BlockSpec((MLA_HEADS, tm, QK_PAD), lambda i: (0, i, 0)),
        out_shape=jax.ShapeDtypeStruct((MLA_HEADS, s_len, QK_PAD), BF16),
        compiler_params=_params(("parallel",)),
        name="q_proj",
    )(h, wdq, qn, wa, wb, cosp, sinp)


def _kv_proj_kernel(h_ref, wd_ref, kn_ref, wu_ref, cos_ref, sin_ref, k_ref, v_ref):
    ckv = jnp.dot(h_ref[...].astype(BF16), wd_ref[...], preferred_element_type=F32)
    c = _rms_norm(ckv[:, :KV_LORA_RANK], kn_ref[...]).astype(BF16)
    k_pe = (ckv[:, KV_LORA_RANK:KV_LORA_RANK + LANES] * cos_ref[...]
            + ckv[:, KV_LORA_RANK + LANES:KV_LORA_RANK + 2 * LANES] * sin_ref[...]).astype(BF16)
    kv = jnp.dot(c, wu_ref[...], preferred_element_type=F32)
    w = QK_NOPE_DIM + V_HEAD_DIM
    for hd in range(MLA_HEADS):
        k_ref[hd, :, 0:LANES] = kv[:, hd * w:hd * w + QK_NOPE_DIM].astype(BF16)
        k_ref[hd, :, LANES:QK_PAD] = k_pe
        v_ref[hd] = kv[:, hd * w + QK_NOPE_DIM:(hd + 1) * w].astype(BF16)


def _kv_proj(h, wd, kn, wu, cosp, sinp, tm):
    s_len = h.shape[0]
    full = lambda i: (0, 0)
    return pl.pallas_call(
        _kv_proj_kernel,
        grid=(s_len // tm,),
        in_specs=[pl.BlockSpec((tm, D_MODEL), lambda i: (i, 0)),
                  pl.BlockSpec(wd.shape, full),
                  pl.BlockSpec(kn.shape, full),
                  pl.BlockSpec(wu.shape, full),
                  pl.BlockSpec((tm, LANES), lambda i: (i, 0)),
                  pl.BlockSpec((tm, LANES), lambda i: (i, 0))],
        out_specs=[pl.BlockSpec((MLA_HEADS, tm, QK_PAD), lambda i: (0, i, 0)),
                   pl.BlockSpec((MLA_HEADS, tm, V_HEAD_DIM), lambda i: (0, i, 0))],
        out_shape=[jax.ShapeDtypeStruct((MLA_HEADS, s_len, QK_PAD), BF16),
                   jax.ShapeDtypeStruct((MLA_HEADS, s_len, V_HEAD_DIM), BF16)],
        compiler_params=_params(("parallel",)),
        name="kv_proj",
    )(h, wd, kn, wu, cosp, sinp)


def _attn_kernel(q_ref, k_ref, v_ref, o_ref, vaug_ref, m_ref, acc_ref):
    tq = q_ref.shape[1]
    i = pl.program_id(1)

    @pl.when(i == 0)
    def _():
        vaug_ref[:, 0:V_HEAD_DIM] = v_ref[0]
        vaug_ref[:, V_HEAD_DIM:] = jnp.ones((vaug_ref.shape[0], LANES), BF16)

    q = q_ref[0]
    m_ref[...] = jnp.full(m_ref.shape, MASK_VALUE, F32)
    acc_ref[...] = jnp.zeros_like(acc_ref)

    def chunk(j, masked):
        start = pl.multiple_of(j * tq, tq)
        kc = k_ref[0, pl.ds(start, tq), :]
        s = lax.dot_general(q, kc, NT_DIMS, preferred_element_type=F32)
        if masked:
            row = lax.broadcasted_iota(I32, s.shape, 0)
            col = lax.broadcasted_iota(I32, s.shape, 1)
            s = jnp.where(col <= row, s, MASK_VALUE)
        m_prev = m_ref[...]
        m_new = jnp.maximum(m_prev, jnp.max(s, axis=1, keepdims=True))
        alpha = jnp.exp(m_prev - m_new)
        p = jnp.exp(s - jnp.concatenate([m_new] * (tq // LANES), axis=1))
        pv = jnp.dot(p.astype(BF16), vaug_ref[pl.ds(start, tq), :], preferred_element_type=F32)
        acc_ref[...] = acc_ref[...] * jnp.concatenate([alpha, alpha], axis=1) + pv
        m_ref[...] = m_new

    def body(j, carry):
        chunk(j, False)
        return carry

    lax.fori_loop(0, i, body, 0)
    chunk(i, True)
    acc = acc_ref[...]
    o_ref[...] = (acc[:, 0:V_HEAD_DIM] / acc[:, V_HEAD_DIM:]).astype(o_ref.dtype)


def _attention(q, k, v, tq):
    s_len = q.shape[1]
    return pl.pallas_call(
        _attn_kernel,
        grid=(MLA_HEADS, s_len // tq),
        in_specs=[pl.BlockSpec((1, tq, QK_PAD), lambda h, i: (h, i, 0)),
                  pl.BlockSpec((1, s_len, QK_PAD), lambda h, i: (h, 0, 0)),
                  pl.BlockSpec((1, s_len, V_HEAD_DIM), lambda h, i: (h, 0, 0))],
        out_specs=pl.BlockSpec((tq, V_HEAD_DIM), lambda h, i: (i, h)),
        out_shape=jax.ShapeDtypeStruct((s_len, MLA_HEADS * V_HEAD_DIM), BF16),
        scratch_shapes=[pltpu.VMEM((s_len, 2 * LANES), BF16),
                        pltpu.VMEM((tq, LANES), F32),
                        pltpu.VMEM((tq, 2 * LANES), F32)],
        compiler_params=_params(("parallel", "arbitrary")),
        name="attention",
    )(q, k, v)


def _route_kernel(x_ref, wh_ref, wl_ref, b_ref, idx_ref, gate_ref, rank_ref, cnt_ref, carry_ref):
    tb = x_ref.shape[0]

    @pl.when(pl.program_id(0) == 0)
    def _():
        carry_ref[...] = jnp.zeros_like(carry_ref)

    x = x_ref[...]
    xh = x.astype(BF16)
    xl = (x - xh.astype(F32)).astype(BF16)
    wh = wh_ref[...]
    logits = (lax.dot_general(wh, xh, NT_DIMS, preferred_element_type=F32)
              + lax.dot_general(wh, xl, NT_DIMS, preferred_element_type=F32)
              + lax.dot_general(wl_ref[...], xh, NT_DIMS, preferred_element_type=F32)
              + b_ref[...])
    erow = lax.broadcasted_iota(I32, logits.shape, 0).astype(F32)
    vals, sels = [], []
    rem = logits
    for _ in range(TOP_K):
        mx = jnp.max(rem, axis=0, keepdims=True)
        ix = jnp.min(jnp.where(rem == mx, erow, float(N_EXPERTS)), axis=0, keepdims=True)
        sel = erow == ix
        rem = jnp.where(sel, NEG_BIG, rem)
        vals.append(mx)
        sels.append(sel)
        idx_ref[len(vals) - 1:len(vals), :] = ix.astype(I32)
    es = [jnp.exp(v - vals[0]) for v in vals]
    den = es[0] + es[1] + es[2] + es[3]
    onehot = jnp.zeros(logits.shape, F32)
    for k in range(TOP_K):
        gate_ref[k:k + 1, :] = es[k] / den
        onehot = onehot + sels[k].astype(F32)
    before = lax.broadcasted_iota(I32, (tb, tb), 0) < lax.broadcasted_iota(I32, (tb, tb), 1)
    excl = jnp.dot(onehot.astype(BF16), before.astype(BF16), preferred_element_type=F32)
    excl = excl + carry_ref[:, 0:1]
    for k in range(TOP_K):
        rank_ref[k:k + 1, :] = jnp.sum(jnp.where(sels[k], excl, 0.0), axis=0, keepdims=True).astype(I32)
    carry_ref[...] = carry_ref[...] + jnp.sum(onehot, axis=1, keepdims=True)
    cnt_ref[...] = carry_ref[...]


def _route(h, wt_hi, wt_lo, bias, tb):
    s_len = h.shape[0]
    full = lambda i: (0, 0)
    tok = pl.BlockSpec((TOP_K, tb), lambda i: (0, i))
    return pl.pallas_call(
        _route_kernel,
        grid=(s_len // tb,),
        in_specs=[pl.BlockSpec((tb, D_MODEL), lambda i: (i, 0)),
                  pl.BlockSpec(wt_hi.shape, full),
                  pl.BlockSpec(wt_lo.shape, full),
                  pl.BlockSpec(bias.shape, full)],
        out_specs=[tok, tok, tok, pl.BlockSpec((N_EXPERTS, LANES), full)],
        out_shape=[jax.ShapeDtypeStruct((TOP_K, s_len), I32),
                   jax.ShapeDtypeStruct((TOP_K, s_len), F32),
                   jax.ShapeDtypeStruct((TOP_K, s_len), I32),
                   jax.ShapeDtypeStruct((N_EXPERTS, LANES), F32)],
        scratch_shapes=[pltpu.VMEM((N_EXPERTS, LANES), F32)],
        compiler_params=_params(("arbitrary",)),
        name="route",
    )(h, wt_hi, wt_lo, bias)


def _pack_rows(x):
    half = x.shape[1] // 2
    lo = lax.bitcast_convert_type(x[:, :half].astype(BF16).astype(F32), U32)
    hi = lax.bitcast_convert_type(x[:, half:].astype(BF16).astype(F32), U32)
    return (hi & jnp.uint32(0xFFFF0000)) | (lo >> 16)


def _unpack_rows(w):
    lo = lax.bitcast_convert_type(w << 16, F32)
    hi = lax.bitcast_convert_type(w & jnp.uint32(0xFFFF0000), F32)
    return jnp.concatenate([lo, hi], axis=1).astype(BF16)


def _dispatch_kernel(last_ref, pos_ref, x_ref, xs_ref, buf_ref, zero_ref, sem_ref, zsem_ref, *, rb):
    i = pl.program_id(0)
    n_steps = pl.num_programs(0)
    td = x_ref.shape[0]
    n_rows = TOP_K * td
    slot = i % 2

    def row_copy(sl, t, dst_row):
        return pltpu.make_async_copy(buf_ref.at[sl, pl.ds(t, 1)], xs_ref.at[pl.ds(dst_row, 1)],
                                     sem_ref.at[sl])

    def wait_rows(sl):
        for _ in range(n_rows):
            row_copy(sl, 0, 0).wait()

    @pl.when(i == 0)
    def _():
        zero_ref[...] = jnp.zeros_like(zero_ref)

        def zero_copy(e):
            return pltpu.make_async_copy(zero_ref, xs_ref.at[pl.ds(last_ref[e] * rb, rb)], zsem_ref)

        for e in range(N_EXPERTS):
            @pl.when(last_ref[e] >= 0)
            def _():
                zero_copy(e).start()
        for e in range(N_EXPERTS):
            @pl.when(last_ref[e] >= 0)
            def _():
                zero_copy(e).wait()

    @pl.when(i >= 2)
    def _():
        wait_rows(slot)

    buf_ref[slot] = _pack_rows(x_ref[...])
    for t in range(td):
        for k in range(TOP_K):
            row_copy(slot, t, pos_ref[0, 0, k * td + t]).start()

    @pl.when(i == n_steps - 1)
    def _():
        wait_rows(slot)

        @pl.when(n_steps >= 2)
        def _():
            wait_rows(1 - slot)


def _dispatch(h, pos_blocks, last_blk, n_rows_total, rb, td):
    s_len = h.shape[0]
    half = D_MODEL // 2
    grid_spec = pltpu.PrefetchScalarGridSpec(
        num_scalar_prefetch=1,
        grid=(s_len // td,),
        in_specs=[pl.BlockSpec((1, 1, TOP_K * td), lambda i, last: (i, 0, 0), memory_space=pltpu.SMEM),
                  pl.BlockSpec((td, D_MODEL), lambda i, last: (i, 0))],
        out_specs=pl.BlockSpec(memory_space=pl.ANY),
        scratch_shapes=[pltpu.VMEM((2, td, half), U32),
                        pltpu.VMEM((rb, half), U32),
                        pltpu.SemaphoreType.DMA((2,)),
                        pltpu.SemaphoreType.DMA(())],
    )
    return pl.pallas_call(
        functools.partial(_dispatch_kernel, rb=rb),
        grid_spec=grid_spec,
        out_shape=jax.ShapeDtypeStruct((n_rows_total, half), U32),
        compiler_params=_params(("arbitrary",)),
        name="dispatch",
    )(last_blk, pos_blocks, h)


def _cast_weight(src_ref, dst_ref, rows_per_step):
    n_steps = dst_ref.shape[0] // rows_per_step

    def body(c, carry):
        r0 = pl.multiple_of(c * rows_per_step, rows_per_step)
        dst_ref[pl.ds(r0, rows_per_step), :] = src_ref[0, pl.ds(r0, rows_per_step), :].astype(BF16)
        return carry

    lax.fori_loop(0, n_steps, body, 0)


def _expert_changed(be_ref, b):
    return jnp.logical_or(b == 0, be_ref[b] != be_ref[jnp.maximum(b - 1, 0)])


def _up_kernel(be_ref, src_ref, nused_ref, xs_ref, w1_ref, b1_ref, act_ref, w1b_ref):
    b = pl.program_id(0)

    @pl.when(b < nused_ref[0])
    def _():
        @pl.when(_expert_changed(be_ref, b))
        def _():
            _cast_weight(w1_ref, w1b_ref, 256)

        x = _unpack_rows(xs_ref[...])
        hh = jnp.dot(x, w1b_ref[...], preferred_element_type=F32) + b1_ref[0]
        glu = jnp.minimum(hh[:, :D_EXPERT], SWIGLU_LIMIT)
        lin = jnp.clip(hh[:, D_EXPERT:], -SWIGLU_LIMIT, SWIGLU_LIMIT)
        act = glu * jax.nn.sigmoid(SWIGLU_ALPHA * glu) * (lin + 1.0)
        act_ref[...] = act.astype(act_ref.dtype)


def _down_kernel(be_ref, src_ref, nused_ref, act_ref, w2_ref, b2_ref, y_ref, w2b_ref):
    b = pl.program_id(0)

    @pl.when(b < nused_ref[0])
    def _():
        @pl.when(_expert_changed(be_ref, b))
        def _():
            _cast_weight(w2_ref, w2b_ref, 256)

        y_ref[...] = jnp.dot(act_ref[...], w2b_ref[...], preferred_element_type=F32) + b2_ref[0]


def _experts(xs, block_e, block_src, n_used, w1, b1, w2, b2, rb):
    n_rows = xs.shape[0]
    nb = n_rows // rb
    half = D_MODEL // 2
    rows = lambda b, be, src, nu: (src[b], 0)
    wsel = lambda b, be, src, nu: (be[b], 0, 0)
    act = pl.pallas_call(
        _up_kernel,
        grid_spec=pltpu.PrefetchScalarGridSpec(
            num_scalar_prefetch=3, grid=(nb,),
            in_specs=[pl.BlockSpec((rb, half), rows),
                      pl.BlockSpec((1, D_MODEL, 2 * D_EXPERT), wsel),
                      pl.BlockSpec((1, 1, 2 * D_EXPERT), wsel)],
            out_specs=pl.BlockSpec((rb, D_EXPERT), rows),
            scratch_shapes=[pltpu.VMEM((D_MODEL, 2 * D_EXPERT), BF16)]),
        out_shape=jax.ShapeDtypeStruct((n_rows, D_EXPERT), BF16),
        compiler_params=_params(("arbitrary",)),
        name="moe_up",
    )(block_e, block_src, n_used, xs, w1, b1)
    return pl.pallas_call(
        _down_kernel,
        grid_spec=pltpu.PrefetchScalarGridSpec(
            num_scalar_prefetch=3, grid=(nb,),
            in_specs=[pl.BlockSpec((rb, D_EXPERT), rows),
                      pl.BlockSpec((1, D_EXPERT, D_MODEL), wsel),
                      pl.BlockSpec((1, 1, D_MODEL), wsel)],
            out_specs=pl.BlockSpec((rb, D_MODEL), rows),
            scratch_shapes=[pltpu.VMEM((D_EXPERT, D_MODEL), BF16)]),
        out_shape=jax.ShapeDtypeStruct((n_rows, D_MODEL), F32),
        compiler_params=_params(("arbitrary",)),
        name="moe_down",
    )(block_e, block_src, n_used, act, w2, b2)


def _combine_kernel(pos_ref, posn_ref, h_ref, gate_ref, p_ref, g_ref, b_ref, wp_ref, wg_ref, yb_ref,
                    o_ref, rows_ref, sem_ref):
    i = pl.program_id(0)
    n_steps = pl.num_programs(0)
    tc = h_ref.shape[0]
    slot = i % 2

    def row_copy(sl, k, t, src_row):
        return pltpu.make_async_copy(yb_ref.at[pl.ds(src_row, 1)], rows_ref.at[sl, k, pl.ds(t, 1)],
                                     sem_ref.at[sl])

    def fetch(sl, p_ref_):
        for t in range(tc):
            for k in range(TOP_K):
                row_copy(sl, k, t, p_ref_[0, 0, k * tc + t]).start()

    @pl.when(i == 0)
    def _():
        fetch(slot, pos_ref)

    @pl.when(i + 1 < n_steps)
    def _():
        fetch(1 - slot, posn_ref)

    for _ in range(TOP_K * tc):
        row_copy(slot, 0, 0, 0).wait()

    gates = gate_ref[...]
    ffn = gates[:, 0:1] * rows_ref[slot, 0]
    for k in range(1, TOP_K):
        ffn = ffn + gates[:, k:k + 1] * rows_ref[slot, k]
    h2 = _layer_norm(DEEPNORM_ALPHA * h_ref[...] + ffn, g_ref[...], b_ref[...])
    emb = jnp.dot(p_ref[...].astype(BF16), wp_ref[...], preferred_element_type=F32)
    gate = jax.nn.sigmoid(jnp.dot(h2.astype(BF16), wg_ref[...], preferred_element_type=F32))
    o_ref[...] = h2 + emb * gate


def _combine(h, yb, pos_blocks, gates_t, p, g, b, wp, wg, tc):
    s_len = h.shape[0]
    n_steps = s_len // tc
    full = lambda i: (0, 0)
    smem_blk = (1, 1, TOP_K * tc)
    return pl.pallas_call(
        _combine_kernel,
        grid=(n_steps,),
        in_specs=[pl.BlockSpec(smem_blk, lambda i: (i, 0, 0), memory_space=pltpu.SMEM),
                  pl.BlockSpec(smem_blk, lambda i: (jnp.minimum(i + 1, n_steps - 1), 0, 0),
                               memory_space=pltpu.SMEM),
                  pl.BlockSpec((tc, D_MODEL), lambda i: (i, 0)),
                  pl.BlockSpec((tc, TOP_K), lambda i: (i, 0)),
                  pl.BlockSpec((tc, PLE_DIM), lambda i: (i, 0)),
                  pl.BlockSpec((1, D_MODEL), full),
                  pl.BlockSpec((1, D_MODEL), full),
                  pl.BlockSpec(wp.shape, full),
                  pl.BlockSpec(wg.shape, full),
                  pl.BlockSpec(memory_space=pl.ANY)],
        out_specs=pl.BlockSpec((tc, D_MODEL), lambda i: (i, 0)),
        out_shape=jax.ShapeDtypeStruct((s_len, D_MODEL), F32),
        scratch_shapes=[pltpu.VMEM((2, TOP_K, tc, D_MODEL), F32),
                        pltpu.SemaphoreType.DMA((2,))],
        compiler_params=_params(("arbitrary",)),
        name="moe_combine",
    )(pos_blocks, pos_blocks, h, gates_t, p, g, b, wp, wg, yb)


def _token_blocks(a, tb):
    s_len = a.shape[1]
    return a.reshape(TOP_K, s_len // tb, tb).transpose(1, 0, 2).reshape(s_len // tb, 1, TOP_K * tb)


ROW_BLOCK = 256
DISPATCH_TOKENS = 128
COMBINE_TOKENS = 256
ROUTE_TOKENS = 512


def _moe_ln_ple(h, p, router_w, router_b, w1, b1, w2, b2, ln_g, ln_b, wp, wg):
    s_len = h.shape[0]
    rb = ROW_BLOCK
    wt = router_w.T
    wt_hi = wt.astype(BF16)
    wt_lo = (wt - wt_hi.astype(F32)).astype(BF16)
    idx, gates, rank, cnt = _route(h, wt_hi, wt_lo, router_b.reshape(N_EXPERTS, 1),
                                   min(ROUTE_TOKENS, s_len))

    counts = cnt[:, 0].astype(I32)
    nblk = (counts + rb - 1) // rb
    blk_end = jnp.cumsum(nblk)
    blk_start = blk_end - nblk
    n_blocks = (TOP_K * s_len + rb - 1) // rb + N_EXPERTS
    n_used = blk_end[-1]
    blk_ids = jnp.minimum(jnp.arange(n_blocks, dtype=I32), n_used - 1)
    block_e = jnp.minimum(jnp.searchsorted(blk_end, blk_ids, side='right'), N_EXPERTS - 1).astype(I32)
    last_blk = jnp.where(nblk > 0, blk_end - 1, -1).astype(I32)
    pos = (blk_start * rb)[idx] + rank

    td = min(DISPATCH_TOKENS, s_len)
    xs = _dispatch(h, _token_blocks(pos, td), last_blk, n_blocks * rb, rb, td)
    yb = _experts(xs, block_e, blk_ids, n_used.reshape(1).astype(I32), w1,
                  b1.reshape(N_EXPERTS, 1, -1), w2, b2.reshape(N_EXPERTS, 1, -1), rb)
    tc = min(COMBINE_TOKENS, s_len)
    return _combine(h, yb, _token_blocks(pos, tc), gates.T, p, ln_g, ln_b, wp, wg, tc)


def _rope_tables(positions):
    inv = 1.0 / (ROPE_THETA ** (jnp.arange(0, QK_ROPE_DIM, 2, dtype=F32) / QK_ROPE_DIM))
    ang = positions.astype(F32)[:, None] * inv
    cos, sin = jnp.cos(ang), jnp.sin(ang)
    zeros = jnp.zeros((positions.shape[0], LANES - QK_ROPE_DIM), F32)
    return (jnp.concatenate([cos, cos, zeros], axis=1), jnp.concatenate([-sin, sin, zeros], axis=1))


def _swap_halves(w):
    half = QK_ROPE_DIM // 2
    return jnp.concatenate([w[..., half:], w[..., :half]], axis=-1)


def kernel(x, p, positions, ln_mix_g, ln_mix_b, ln_ffn_g, ln_ffn_b, rg_w_in, rg_conv_w, rg_conv_b, rg_gate_a_w, rg_gate_a_b, rg_gate_x_w, rg_gate_x_b, rg_lambda, rg_w_out, mla_w_dq, mla_q_norm, mla_w_uq, mla_w_o, kv_w_dkv, kv_norm, kv_w_ukv, moe_router_w, moe_router_b, moe_w1, moe_b1, moe_w2, moe_b2, ple_w_proj, ple_w_gate):
    bsz, s_len, _ = x.shape
    assert bsz == 1
    h = x.reshape(s_len, D_MODEL)
    cosp, sinp = _rope_tables(positions.reshape(s_len))
    tm = min(512, s_len)
    k_heads = v_heads = None
    for i in range(DEPTH):
        if i < N_A_LAYERS:
            u = _linear(h, rg_w_in[i].astype(BF16), F32, min(1024, s_len), 1024)
            yg = _rglru(u, rg_conv_w[i], rg_conv_b[i].reshape(1, D_RNN),
                        rg_gate_a_w[i].astype(BF16), rg_gate_a_b[i].reshape(RG_HEADS, 1, RG_BLOCK),
                        rg_gate_x_w[i].astype(BF16), rg_gate_x_b[i].reshape(RG_HEADS, 1, RG_BLOCK),
                        rg_lambda[i].reshape(1, D_RNN), min(256, s_len))
            w_o = rg_w_out[i]
        else:
            j = i - N_A_LAYERS
            if k_heads is None:
                lat, rope = kv_w_dkv[:, :KV_LORA_RANK], kv_w_dkv[:, KV_LORA_RANK:]
                zc = jnp.zeros((D_MODEL, LANES - QK_ROPE_DIM), F32)
                wd = jnp.concatenate([lat, rope, zc, _swap_halves(rope), zc], axis=1).astype(BF16)
                k_heads, v_heads = _kv_proj(h, wd, kv_norm.reshape(1, -1), kv_w_ukv.astype(BF16),
                                            cosp, sinp, min(256, s_len))
            wq = mla_w_uq[j].reshape(Q_LORA_RANK, MLA_HEADS, QK_NOPE_DIM + QK_ROPE_DIM)
            nope, rope = wq[..., :QK_NOPE_DIM], wq[..., QK_NOPE_DIM:]
            zq = jnp.zeros((Q_LORA_RANK, MLA_HEADS, LANES - QK_ROPE_DIM), F32)
            wa = jnp.concatenate([nope, rope, zq], axis=-1).reshape(Q_LORA_RANK, MLA_HEADS * QK_PAD)
            wb = jnp.concatenate([_swap_halves(rope), zq], axis=-1).reshape(Q_LORA_RANK, MLA_HEADS * LANES)
            q_heads = _q_proj(h, mla_w_dq[j].astype(BF16), mla_q_norm[j].reshape(1, -1),
                              wa.astype(BF16), wb.astype(BF16), cosp, sinp, min(256, s_len))
            yg = _attention(q_heads, k_heads, v_heads, min(512, s_len))
            w_o = mla_w_o[j]
        h = _proj_ln(yg, w_o.astype(BF16), h, ln_mix_g[i].reshape(1, -1), ln_mix_b[i].reshape(1, -1), tm)
        h = _moe_ln_ple(h, p[i].reshape(s_len, PLE_DIM), moe_router_w[i], moe_router_b[i],
                        moe_w1[i], moe_b1[i], moe_w2[i], moe_b2[i],
                        ln_ffn_g[i].reshape(1, -1), ln_ffn_b[i].reshape(1, -1),
                        ple_w_proj[i].astype(BF16), ple_w_gate[i].astype(BF16))
    return h.reshape(bsz, s_len, D_MODEL)
```

```python
import functools

import jax
import jax.numpy as jnp
from jax import lax
from jax.experimental import pallas as pl
from jax.experimental.pallas import tpu as pltpu

F32 = jnp.float32
BF16 = jnp.bfloat16
I32 = jnp.int32
U32 = jnp.uint32

D_MODEL = 2048
DEPTH = 4
N_A_LAYERS = DEPTH // 2
DEEPNORM_ALPHA = (2 * DEPTH) ** 0.25
LN_EPS = 1e-5
RMS_EPS = 1e-6
D_RNN = D_MODEL
RG_HEADS = 8
RG_BLOCK = D_RNN // RG_HEADS
CONV_WIDTH = 4
RG_C = 8.0
MLA_HEADS = 16
Q_LORA_RANK = 768
KV_LORA_RANK = 512
QK_NOPE_DIM = 128
QK_ROPE_DIM = 64
V_HEAD_DIM = 128
ROPE_THETA = 10000.0
ATTN_SCALE = (QK_NOPE_DIM + QK_ROPE_DIM) ** -0.5
N_EXPERTS = 32
TOP_K = 4
D_EXPERT = 1024
SWIGLU_LIMIT = 7.0
SWIGLU_ALPHA = 1.702
PLE_DIM = 256

LANES = 128
SUBLANES = 8
QK_PAD = 2 * LANES
MASK_VALUE = -1e30
NEG_BIG = -3.0e38

VMEM_LIMIT = 56 * 1024 * 1024

NT_DIMS = (((1,), (1,)), ((), ()))


def _params(sem, vmem=VMEM_LIMIT):
    return pltpu.CompilerParams(dimension_semantics=sem, vmem_limit_bytes=vmem)


def _layer_norm(z, g, b):
    mu = jnp.mean(z, axis=-1, keepdims=True)
    zc = z - mu
    var = jnp.mean(zc * zc, axis=-1, keepdims=True)
    return zc * lax.rsqrt(var + LN_EPS) * g + b


def _rms_norm(x, g):
    return x * lax.rsqrt(jnp.mean(x * x, axis=-1, keepdims=True) + RMS_EPS) * g


def _linear_kernel(x_ref, w_ref, o_ref):
    x = x_ref[...].astype(BF16)
    o_ref[...] = jnp.dot(x, w_ref[...], preferred_element_type=F32).astype(o_ref.dtype)


def _linear(x, w, out_dtype, tm, tn):
    m, k = x.shape
    n = w.shape[1]
    return pl.pallas_call(
        _linear_kernel,
        grid=(m // tm, n // tn),
        in_specs=[pl.BlockSpec((tm, k), lambda i, j: (i, 0)),
                  pl.BlockSpec((k, tn), lambda i, j: (0, j))],
        out_specs=pl.BlockSpec((tm, tn), lambda i, j: (i, j)),
        out_shape=jax.ShapeDtypeStruct((m, n), out_dtype),
        compiler_params=_params(("parallel", "parallel")),
        name="linear",
    )(x, w)


def _proj_ln_kernel(x_ref, w_ref, h_ref, g_ref, b_ref, o_ref):
    mix = jnp.dot(x_ref[...].astype(BF16), w_ref[...], preferred_element_type=F32)
    z = DEEPNORM_ALPHA * h_ref[...] + mix
    o_ref[...] = _layer_norm(z, g_ref[...], b_ref[...])


def _proj_ln(x, w, h, g, b, tm):
    m, k = x.shape
    n = w.shape[1]
    return pl.pallas_call(
        _proj_ln_kernel,
        grid=(m // tm,),
        in_specs=[pl.BlockSpec((tm, k), lambda i: (i, 0)),
                  pl.BlockSpec((k, n), lambda i: (0, 0)),
                  pl.BlockSpec((tm, n), lambda i: (i, 0)),
                  pl.BlockSpec((1, n), lambda i: (0, 0)),
                  pl.BlockSpec((1, n), lambda i: (0, 0))],
        out_specs=pl.BlockSpec((tm, n), lambda i: (i, 0)),
        out_shape=jax.ShapeDtypeStruct((m, n), F32),
        compiler_params=_params(("parallel",)),
        name="proj_ln",
    )(x, w, h, g, b)


def _shift_rows(x, d, fill_rows):
    xr = pltpu.roll(x, d, 0)
    row8 = lax.broadcasted_iota(I32, fill_rows.shape, 0)
    top = jnp.where(row8 < d, fill_rows, xr[:SUBLANES])
    return jnp.concatenate([top, xr[SUBLANES:]], axis=0)


def _rglru_kernel(gate_ref, rec_ref, cw_ref, cb_ref, wa_ref, ba_ref, wx_ref, bx_ref, lam_ref,
                  o_ref, tail_ref, state_ref):
    t_blk = rec_ref.shape[0]

    @pl.when(pl.program_id(1) == 0)
    def _():
        tail_ref[...] = jnp.zeros_like(tail_ref)
        state_ref[...] = jnp.zeros_like(state_ref)

    x = rec_ref[...]
    prev = tail_ref[...]
    conv = cb_ref[...] + cw_ref[CONV_WIDTH - 1:CONV_WIDTH, :] * x
    for d in range(1, CONV_WIDTH):
        shifted = _shift_rows(x, d, pltpu.roll(prev, d, 0))
        conv = conv + cw_ref[CONV_WIDTH - 1 - d:CONV_WIDTH - d, :] * shifted
    tail_ref[...] = x[t_blk - SUBLANES:]

    xb = conv.astype(BF16)
    r = jax.nn.sigmoid(jnp.dot(xb, wa_ref[0], preferred_element_type=F32) + ba_ref[0])
    ig = jax.nn.sigmoid(jnp.dot(xb, wx_ref[0], preferred_element_type=F32) + bx_ref[0])
    lam = lam_ref[...]
    log_sig = jnp.minimum(lam, 0.0) - jnp.log1p(jnp.exp(-jnp.abs(lam)))
    log_a = RG_C * r * log_sig
    a = jnp.exp(log_a)
    th = jnp.tanh(log_a)
    b = jnp.sqrt(-2.0 * th / (1.0 - th)) * (ig * conv)

    ones8 = jnp.ones((SUBLANES, x.shape[1]), F32)
    zeros8 = jnp.zeros((SUBLANES, x.shape[1]), F32)
    s = 1
    while s < t_blk:
        if s < SUBLANES:
            a_sh = _shift_rows(a, s, ones8)
            b_sh = _shift_rows(b, s, zeros8)
        else:
            a_sh = jnp.concatenate([jnp.ones((s, x.shape[1]), F32), a[:t_blk - s]], axis=0)
            b_sh = jnp.concatenate([jnp.zeros((s, x.shape[1]), F32), b[:t_blk - s]], axis=0)
        b = a * b_sh + b
        a = a * a_sh
        s *= 2
    h = a * state_ref[SUBLANES - 1:SUBLANES, :] + b
    state_ref[...] = h[t_blk - SUBLANES:]

    o_ref[...] = (jax.nn.gelu(gate_ref[...], approximate=True) * h).astype(o_ref.dtype)


def _rglru(u, conv_w, conv_b, wa, ba, wx, bx, lam, t_blk):
    s_len = u.shape[0]
    c = RG_BLOCK
    return pl.pallas_call(
        _rglru_kernel,
        grid=(RG_HEADS, s_len // t_blk),
        in_specs=[pl.BlockSpec((t_blk, c), lambda h, t: (t, h)),
                  pl.BlockSpec((t_blk, c), lambda h, t: (t, RG_HEADS + h)),
                  pl.BlockSpec((CONV_WIDTH, c), lambda h, t: (0, h)),
                  pl.BlockSpec((1, c), lambda h, t: (0, h)),
                  pl.BlockSpec((1, c, c), lambda h, t: (h, 0, 0)),
                  pl.BlockSpec((1, 1, c), lambda h, t: (h, 0, 0)),
                  pl.BlockSpec((1, c, c), lambda h, t: (h, 0, 0)),
                  pl.BlockSpec((1, 1, c), lambda h, t: (h, 0, 0)),
                  pl.BlockSpec((1, c), lambda h, t: (0, h))],
        out_specs=pl.BlockSpec((t_blk, c), lambda h, t: (t, h)),
        out_shape=jax.ShapeDtypeStruct((s_len, D_RNN), BF16),
        scratch_shapes=[pltpu.VMEM((SUBLANES, c), F32), pltpu.VMEM((SUBLANES, c), F32)],
        compiler_params=_params(("parallel", "arbitrary")),
        name="rglru",
    )(u, u, conv_w, conv_b, wa, ba, wx, bx, lam)


def _q_proj_kernel(h_ref, wdq_ref, qn_ref, wa_ref, wb_ref, cos_ref, sin_ref, q_ref):
    cq = jnp.dot(h_ref[...].astype(BF16), wdq_ref[...], preferred_element_type=F32)
    cq = _rms_norm(cq, qn_ref[...]).astype(BF16)
    qa = jnp.dot(cq, wa_ref[...], preferred_element_type=F32)
    qb = jnp.dot(cq, wb_ref[...], preferred_element_type=F32)
    cosp = cos_ref[...]
    sinp = sin_ref[...]
    for hd in range(MLA_HEADS):
        nope = qa[:, hd * QK_PAD:hd * QK_PAD + LANES]
        rope = qa[:, hd * QK_PAD + LANES:(hd + 1) * QK_PAD] * cosp + qb[:, hd * LANES:(hd + 1) * LANES] * sinp
        q_ref[hd, :, 0:LANES] = (nope * ATTN_SCALE).astype(BF16)
        q_ref[hd, :, LANES:QK_PAD] = (rope * ATTN_SCALE).astype(BF16)


def _q_proj(h, wdq, qn, wa, wb, cosp, sinp, tm):
    s_len = h.shape[0]
    full = lambda i: (0, 0)
    return pl.pallas_call(
        _q_proj_kernel,
        grid=(s_len // tm,),
        in_specs=[pl.BlockSpec((tm, D_MODEL), lambda i: (i, 0)),
                  pl.BlockSpec(wdq.shape, full),
                  pl.BlockSpec(qn.shape, full),
                  pl.BlockSpec(wa.shape, full),
                  pl.BlockSpec(wb.shape, full),
                  pl.BlockSpec((tm, LANES), lambda i: (i, 0)),
                  pl.BlockSpec((tm, LANES), lambda i: (i, 0))],
        out_specs=pl.BlockSpec((MLA_HEADS, tm, QK_PAD), lambda i: (0, i, 0)),
        out_shape=jax.ShapeDtypeStruct((MLA_HEADS, s_len, QK_PAD), BF16),
        compiler_params=_params(("parallel",)),
        name="q_proj",
    )(h, wdq, qn, wa, wb, cosp, sinp)


def _kv_proj_kernel(h_ref, wd_ref, kn_ref, wu_ref, cos_ref, sin_ref, k_ref, v_ref):
    ckv = jnp.dot(h_ref[...].astype(BF16), wd_ref[...], preferred_element_type=F32)
    c = _rms_norm(ckv[:, :KV_LORA_RANK], kn_ref[...]).astype(BF16)
    k_pe = (ckv[:, KV_LORA_RANK:KV_LORA_RANK + LANES] * cos_ref[...]
            + ckv[:, KV_LORA_RANK + LANES:KV_LORA_RANK + 2 * LANES] * sin_ref[...]).astype(BF16)
    kv = jnp.dot(c, wu_ref[...], preferred_element_type=F32)
    w = QK_NOPE_DIM + V_HEAD_DIM
    for hd in range(MLA_HEADS):
        k_ref[hd, :, 0:LANES] = kv[:, hd * w:hd * w + QK_NOPE_DIM].astype(BF16)
        k_ref[hd, :, LANES:QK_PAD] = k_pe
        v_ref[hd] = kv[:, hd * w + QK_NOPE_DIM:(hd + 1) * w].astype(BF16)


def _kv_proj(h, wd, kn, wu, cosp, sinp, tm):
    s_len = h.shape[0]
    full = lambda i: (0, 0)
    return pl.pallas_call(
        _kv_proj_kernel,
        grid=(s_len // tm,),
        in_specs=[pl.BlockSpec((tm, D_MODEL), lambda i: (i, 0)),
                  pl.BlockSpec(wd.shape, full),
                  pl.BlockSpec(kn.shape, full),
                  pl.BlockSpec(wu.shape, full),
                  pl.BlockSpec((tm, LANES), lambda i: (i, 0)),
                  pl.BlockSpec((tm, LANES), lambda i: (i, 0))],
        out_specs=[pl.BlockSpec((MLA_HEADS, tm, QK_PAD), lambda i: (0, i, 0)),
                   pl.BlockSpec((MLA_HEADS, tm, V_HEAD_DIM), lambda i: (0, i, 0))],
        out_shape=[jax.ShapeDtypeStruct((MLA_HEADS, s_len, QK_PAD), BF16),
                   jax.ShapeDtypeStruct((MLA_HEADS, s_len, V_HEAD_DIM), BF16)],
        compiler_params=_params(("parallel",)),
        name="kv_proj",
    )(h, wd, kn, wu, cosp, sinp)


N_SUB = 2


def _attn_kernel(q_ref, k_ref, v_ref, o_ref, vaug_ref, m_ref, acc_ref):
    t = q_ref.shape[1] // N_SUB
    i = pl.program_id(1)

    @pl.when(i == 0)
    def _():
        vaug_ref[:, 0:V_HEAD_DIM] = v_ref[0]
        vaug_ref[:, V_HEAD_DIM:] = jnp.ones((vaug_ref.shape[0], LANES), BF16)

    m_ref[...] = jnp.full(m_ref.shape, MASK_VALUE, F32)
    acc_ref[...] = jnp.zeros_like(acc_ref)

    def load_chunk(j):
        start = pl.multiple_of(j * t, t)
        return k_ref[0, pl.ds(start, t), :], vaug_ref[pl.ds(start, t), :]

    def update(sub, kc, vc, masked):
        q = q_ref[0, sub * t:(sub + 1) * t, :]
        s = lax.dot_general(q, kc, NT_DIMS, preferred_element_type=F32)
        if masked:
            row = lax.broadcasted_iota(I32, s.shape, 0)
            col = lax.broadcasted_iota(I32, s.shape, 1)
            s = jnp.where(col <= row, s, MASK_VALUE)
        m_prev = m_ref[sub]
        m_new = jnp.maximum(m_prev, jnp.max(s, axis=1, keepdims=True))
        alpha = jnp.exp(m_prev - m_new)
        p = jnp.exp(s - jnp.concatenate([m_new] * (t // LANES), axis=1))
        pv = jnp.dot(p.astype(BF16), vc, preferred_element_type=F32)
        acc_ref[sub] = acc_ref[sub] * jnp.concatenate([alpha, alpha], axis=1) + pv
        m_ref[sub] = m_new

    def body(jj, carry):
        for c in range(N_SUB):
            kc, vc = load_chunk(jj * N_SUB + c)
            for sub in range(N_SUB):
                update(sub, kc, vc, False)
        return carry

    lax.fori_loop(0, i, body, 0)
    for c in range(N_SUB):
        kc, vc = load_chunk(i * N_SUB + c)
        for sub in range(c, N_SUB):
            update(sub, kc, vc, sub == c)
    for sub in range(N_SUB):
        acc = acc_ref[sub]
        o_ref[sub * t:(sub + 1) * t, :] = (acc[:, 0:V_HEAD_DIM] / acc[:, V_HEAD_DIM:]).astype(o_ref.dtype)


def _attention(q, k, v, tq):
    s_len = q.shape[1]
    t = tq // N_SUB
    return pl.pallas_call(
        _attn_kernel,
        grid=(MLA_HEADS, s_len // tq),
        in_specs=[pl.BlockSpec((1, tq, QK_PAD), lambda h, i: (h, i, 0)),
                  pl.BlockSpec((1, s_len, QK_PAD), lambda h, i: (h, 0, 0)),
                  pl.BlockSpec((1, s_len, V_HEAD_DIM), lambda h, i: (h, 0, 0))],
        out_specs=pl.BlockSpec((tq, V_HEAD_DIM), lambda h, i: (i, h)),
        out_shape=jax.ShapeDtypeStruct((s_len, MLA_HEADS * V_HEAD_DIM), BF16),
        scratch_shapes=[pltpu.VMEM((s_len, 2 * LANES), BF16),
                        pltpu.VMEM((N_SUB, t, LANES), F32),
                        pltpu.VMEM((N_SUB, t, 2 * LANES), F32)],
        compiler_params=_params(("parallel", "arbitrary")),
        name="attention",
    )(q, k, v)


def _route_kernel(x_ref, wh_ref, wl_ref, b_ref, idx_ref, gate_ref, rank_ref, cnt_ref, carry_ref):
    tb = x_ref.shape[0]

    @pl.when(pl.program_id(0) == 0)
    def _():
        carry_ref[...] = jnp.zeros_like(carry_ref)

    x = x_ref[...]
    xh = x.astype(BF16)
    xl = (x - xh.astype(F32)).astype(BF16)
    wh = wh_ref[...]
    logits = (lax.dot_general(wh, xh, NT_DIMS, preferred_element_type=F32)
              + lax.dot_general(wh, xl, NT_DIMS, preferred_element_type=F32)
              + lax.dot_general(wl_ref[...], xh, NT_DIMS, preferred_element_type=F32)
              + b_ref[...])
    erow = lax.broadcasted_iota(I32, logits.shape, 0).astype(F32)
    vals, sels = [], []
    rem = logits
    for _ in range(TOP_K):
        mx = jnp.max(rem, axis=0, keepdims=True)
        ix = jnp.min(jnp.where(rem == mx, erow, float(N_EXPERTS)), axis=0, keepdims=True)
        sel = erow == ix
        rem = jnp.where(sel, NEG_BIG, rem)
        vals.append(mx)
        sels.append(sel)
        idx_ref[len(vals) - 1:len(vals), :] = ix.astype(I32)
    es = [jnp.exp(v - vals[0]) for v in vals]
    den = es[0] + es[1] + es[2] + es[3]
    onehot = jnp.zeros(logits.shape, F32)
    for k in range(TOP_K):
        gate_ref[k:k + 1, :] = es[k] / den
        onehot = onehot + sels[k].astype(F32)
    before = lax.broadcasted_iota(I32, (tb, tb), 0) < lax.broadcasted_iota(I32, (tb, tb), 1)
    excl = jnp.dot(onehot.astype(BF16), before.astype(BF16), preferred_element_type=F32)
    excl = excl + carry_ref[:, 0:1]
    for k in range(TOP_K):
        rank_ref[k:k + 1, :] = jnp.sum(jnp.where(sels[k], excl, 0.0), axis=0, keepdims=True).astype(I32)
    carry_ref[...] = carry_ref[...] + jnp.sum(onehot, axis=1, keepdims=True)
    cnt_ref[...] = carry_ref[...]


def _route(h, wt_hi, wt_lo, bias, tb):
    s_len = h.shape[0]
    full = lambda i: (0, 0)
    tok = pl.BlockSpec((TOP_K, tb), lambda i: (0, i))
    return pl.pallas_call(
        _route_kernel,
        grid=(s_len // tb,),
        in_specs=[pl.BlockSpec((tb, D_MODEL), lambda i: (i, 0)),
                  pl.BlockSpec(wt_hi.shape, full),
                  pl.BlockSpec(wt_lo.shape, full),
                  pl.BlockSpec(bias.shape, full)],
        out_specs=[tok, tok, tok, pl.BlockSpec((N_EXPERTS, LANES), full)],
        out_shape=[jax.ShapeDtypeStruct((TOP_K, s_len), I32),
                   jax.ShapeDtypeStruct((TOP_K, s_len), F32),
                   jax.ShapeDtypeStruct((TOP_K, s_len), I32),
                   jax.ShapeDtypeStruct((N_EXPERTS, LANES), F32)],
        scratch_shapes=[pltpu.VMEM((N_EXPERTS, LANES), F32)],
        compiler_params=_params(("arbitrary",)),
        name="route",
    )(h, wt_hi, wt_lo, bias)


def _pack_rows(x):
    half = x.shape[1] // 2
    lo = lax.bitcast_convert_type(x[:, :half].astype(BF16).astype(F32), U32)
    hi = lax.bitcast_convert_type(x[:, half:].astype(BF16).astype(F32), U32)
    return (hi & jnp.uint32(0xFFFF0000)) | (lo >> 16)


def _unpack_rows(w):
    lo = lax.bitcast_convert_type(w << 16, F32)
    hi = lax.bitcast_convert_type(w & jnp.uint32(0xFFFF0000), F32)
    return jnp.concatenate([lo, hi], axis=1).astype(BF16)


def _dispatch_kernel(last_ref, pos_ref, x_ref, xs_ref, buf_ref, zero_ref, sem_ref, zsem_ref, *, rb):
    i = pl.program_id(0)
    n_steps = pl.num_programs(0)
    td = x_ref.shape[0]
    n_rows = TOP_K * td
    slot = i % 2

    def row_copy(sl, t, dst_row):
        return pltpu.make_async_copy(buf_ref.at[sl, pl.ds(t, 1)], xs_ref.at[pl.ds(dst_row, 1)],
                                     sem_ref.at[sl])

    def wait_rows(sl):
        for _ in range(n_rows):
            row_copy(sl, 0, 0).wait()

    @pl.when(i == 0)
    def _():
        zero_ref[...] = jnp.zeros_like(zero_ref)

        def zero_copy(e):
            return pltpu.make_async_copy(zero_ref, xs_ref.at[pl.ds(last_ref[e] * rb, rb)], zsem_ref)

        for e in range(N_EXPERTS):
            @pl.when(last_ref[e] >= 0)
            def _():
                zero_copy(e).start()
        for e in range(N_EXPERTS):
            @pl.when(last_ref[e] >= 0)
            def _():
                zero_copy(e).wait()

    @pl.when(i >= 2)
    def _():
        wait_rows(slot)

    buf_ref[slot] = _pack_rows(x_ref[...])
    for t in range(td):
        for k in range(TOP_K):
            row_copy(slot, t, pos_ref[0, 0, k * td + t]).start()

    @pl.when(i == n_steps - 1)
    def _():
        wait_rows(slot)

        @pl.when(n_steps >= 2)
        def _():
            wait_rows(1 - slot)


def _dispatch(h, pos_blocks, last_blk, n_rows_total, rb, td):
    s_len = h.shape[0]
    half = D_MODEL // 2
    grid_spec = pltpu.PrefetchScalarGridSpec(
        num_scalar_prefetch=1,
        grid=(s_len // td,),
        in_specs=[pl.BlockSpec((1, 1, TOP_K * td), lambda i, last: (i, 0, 0), memory_space=pltpu.SMEM),
                  pl.BlockSpec((td, D_MODEL), lambda i, last: (i, 0))],
        out_specs=pl.BlockSpec(memory_space=pl.ANY),
        scratch_shapes=[pltpu.VMEM((2, td, half), U32),
                        pltpu.VMEM((rb, half), U32),
                        pltpu.SemaphoreType.DMA((2,)),
                        pltpu.SemaphoreType.DMA(())],
    )
    return pl.pallas_call(
        functools.partial(_dispatch_kernel, rb=rb),
        grid_spec=grid_spec,
        out_shape=jax.ShapeDtypeStruct((n_rows_total, half), U32),
        compiler_params=_params(("arbitrary",)),
        name="dispatch",
    )(last_blk, pos_blocks, h)


WEIGHT_CAST_ROWS = 256


def _stream_expert_weight(tab_ref, b, w_hbm, stage_ref, wbf_ref, sem_ref, layer):
    def copy(e, sl):
        return pltpu.make_async_copy(w_hbm.at[layer, e], stage_ref.at[sl], sem_ref.at[sl])

    @pl.when(tab_ref[2, b] == 1)
    def _():
        e = tab_ref[0, b]
        sl = tab_ref[3, b]
        nxt = tab_ref[4, b]

        @pl.when(b == 0)
        def _():
            copy(e, sl).start()

        @pl.when(nxt >= 0)
        def _():
            copy(nxt, 1 - sl).start()

        copy(e, sl).wait()

        def body(c, carry):
            r0 = pl.multiple_of(c * WEIGHT_CAST_ROWS, WEIGHT_CAST_ROWS)
            wbf_ref[pl.ds(r0, WEIGHT_CAST_ROWS), :] = stage_ref[sl, pl.ds(r0, WEIGHT_CAST_ROWS), :].astype(BF16)
            return carry

        lax.fori_loop(0, wbf_ref.shape[0] // WEIGHT_CAST_ROWS, body, 0)


def _up_kernel(tab_ref, nused_ref, xs_ref, b1_ref, w1_hbm, act_ref, stage_ref, w1b_ref, sem_ref, *, layer):
    b = pl.program_id(0)

    @pl.when(b < nused_ref[0])
    def _():
        _stream_expert_weight(tab_ref, b, w1_hbm, stage_ref, w1b_ref, sem_ref, layer)
        x = _unpack_rows(xs_ref[...])
        hh = jnp.dot(x, w1b_ref[...], preferred_element_type=F32) + b1_ref[0, 0]
        glu = jnp.minimum(hh[:, :D_EXPERT], SWIGLU_LIMIT)
        lin = jnp.clip(hh[:, D_EXPERT:], -SWIGLU_LIMIT, SWIGLU_LIMIT)
        act = glu * jax.nn.sigmoid(SWIGLU_ALPHA * glu) * (lin + 1.0)
        act_ref[...] = act.astype(act_ref.dtype)


def _down_kernel(tab_ref, nused_ref, act_ref, b2_ref, w2_hbm, y_ref, stage_ref, w2b_ref, sem_ref, *, layer):
    b = pl.program_id(0)

    @pl.when(b < nused_ref[0])
    def _():
        _stream_expert_weight(tab_ref, b, w2_hbm, stage_ref, w2b_ref, sem_ref, layer)
        y_ref[...] = jnp.dot(act_ref[...], w2b_ref[...], preferred_element_type=F32) + b2_ref[0, 0]


def _experts(xs, block_tab, n_used, w1, b1, w2, b2, layer, rb):
    n_rows = xs.shape[0]
    nb = n_rows // rb
    half = D_MODEL // 2
    rows = lambda b, tab, nu: (tab[1, b], 0)
    bsel = lambda b, tab, nu: (layer, tab[0, b], 0, 0)
    act = pl.pallas_call(
        functools.partial(_up_kernel, layer=layer),
        grid_spec=pltpu.PrefetchScalarGridSpec(
            num_scalar_prefetch=2, grid=(nb,),
            in_specs=[pl.BlockSpec((rb, half), rows),
                      pl.BlockSpec((1, 1, 1, 2 * D_EXPERT), bsel),
                      pl.BlockSpec(memory_space=pl.ANY)],
            out_specs=pl.BlockSpec((rb, D_EXPERT), rows),
            scratch_shapes=[pltpu.VMEM((2, D_MODEL, 2 * D_EXPERT), F32),
                            pltpu.VMEM((D_MODEL, 2 * D_EXPERT), BF16),
                            pltpu.SemaphoreType.DMA((2,))]),
        out_shape=jax.ShapeDtypeStruct((n_rows, D_EXPERT), BF16),
        compiler_params=_params(("arbitrary",)),
        name="moe_up",
    )(block_tab, n_used, xs, b1, w1)
    return pl.pallas_call(
        functools.partial(_down_kernel, layer=layer),
        grid_spec=pltpu.PrefetchScalarGridSpec(
            num_scalar_prefetch=2, grid=(nb,),
            in_specs=[pl.BlockSpec((rb, D_EXPERT), rows),
                      pl.BlockSpec((1, 1, 1, D_MODEL), bsel),
                      pl.BlockSpec(memory_space=pl.ANY)],
            out_specs=pl.BlockSpec((rb, D_MODEL), rows),
            scratch_shapes=[pltpu.VMEM((2, D_EXPERT, D_MODEL), F32),
                            pltpu.VMEM((D_EXPERT, D_MODEL), BF16),
                            pltpu.SemaphoreType.DMA((2,))]),
        out_shape=jax.ShapeDtypeStruct((n_rows, D_MODEL), F32),
        compiler_params=_params(("arbitrary",)),
        name="moe_down",
    )(block_tab, n_used, act, b2, w2)


def _combine_kernel(pos_ref, posn_ref, h_ref, gate_ref, p_ref, g_ref, b_ref, wp_ref, wg_ref, yb_ref,
                    o_ref, rows_ref, sem_ref):
    i = pl.program_id(0)
    n_steps = pl.num_programs(0)
    tc = h_ref.shape[0]
    slot = i % 2

    def row_copy(sl, k, t, src_row):
        return pltpu.make_async_copy(yb_ref.at[pl.ds(src_row, 1)], rows_ref.at[sl, k, pl.ds(t, 1)],
                                     sem_ref.at[sl])

    def fetch(sl, p_ref_):
        for t in range(tc):
            for k in range(TOP_K):
                row_copy(sl, k, t, p_ref_[0, 0, k * tc + t]).start()

    @pl.when(i == 0)
    def _():
        fetch(slot, pos_ref)

    @pl.when(i + 1 < n_steps)
    def _():
        fetch(1 - slot, posn_ref)

    for _ in range(TOP_K * tc):
        row_copy(slot, 0, 0, 0).wait()

    gates = gate_ref[...]
    ffn = gates[:, 0:1] * rows_ref[slot, 0]
    for k in range(1, TOP_K):
        ffn = ffn + gates[:, k:k + 1] * rows_ref[slot, k]
    h2 = _layer_norm(DEEPNORM_ALPHA * h_ref[...] + ffn, g_ref[...], b_ref[...])
    emb = jnp.dot(p_ref[...].astype(BF16), wp_ref[...], preferred_element_type=F32)
    gate = jax.nn.sigmoid(jnp.dot(h2.astype(BF16), wg_ref[...], preferred_element_type=F32))
    o_ref[...] = h2 + emb * gate


def _combine(h, yb, pos_blocks, gates_t, p, g, b, wp, wg, tc):
    s_len = h.shape[0]
    n_steps = s_len // tc
    full = lambda i: (0, 0)
    smem_blk = (1, 1, TOP_K * tc)
    return pl.pallas_call(
        _combine_kernel,
        grid=(n_steps,),
        in_specs=[pl.BlockSpec(smem_blk, lambda i: (i, 0, 0), memory_space=pltpu.SMEM),
                  pl.BlockSpec(smem_blk, lambda i: (jnp.minimum(i + 1, n_steps - 1), 0, 0),
                               memory_space=pltpu.SMEM),
                  pl.BlockSpec((tc, D_MODEL), lambda i: (i, 0)),
                  pl.BlockSpec((tc, TOP_K), lambda i: (i, 0)),
                  pl.BlockSpec((tc, PLE_DIM), lambda i: (i, 0)),
                  pl.BlockSpec((1, D_MODEL), full),
                  pl.BlockSpec((1, D_MODEL), full),
                  pl.BlockSpec(wp.shape, full),
                  pl.BlockSpec(wg.shape, full),
                  pl.BlockSpec(memory_space=pl.ANY)],
        out_specs=pl.BlockSpec((tc, D_MODEL), lambda i: (i, 0)),
        out_shape=jax.ShapeDtypeStruct((s_len, D_MODEL), F32),
        scratch_shapes=[pltpu.VMEM((2, TOP_K, tc, D_MODEL), F32),
                        pltpu.SemaphoreType.DMA((2,))],
        compiler_params=_params(("arbitrary",)),
        name="moe_combine",
    )(pos_blocks, pos_blocks, h, gates_t, p, g, b, wp, wg, yb)


def _token_blocks(a, tb):
    s_len = a.shape[1]
    return a.reshape(TOP_K, s_len // tb, tb).transpose(1, 0, 2).reshape(s_len // tb, 1, TOP_K * tb)


ROW_BLOCK = 256
DISPATCH_TOKENS = 128
COMBINE_TOKENS = 256
ROUTE_TOKENS = 512


def _lookup(table, keys):
    hit = keys[..., None] == jnp.arange(table.shape[0], dtype=I32)
    return jnp.sum(jnp.where(hit, table, 0), axis=-1)


def _moe_ln_ple(h, p, router_w, router_b, w1, b1, w2, b2, layer, ln_g, ln_b, wp, wg):
    s_len = h.shape[0]
    rb = ROW_BLOCK
    wt = router_w.T
    wt_hi = wt.astype(BF16)
    wt_lo = (wt - wt_hi.astype(F32)).astype(BF16)
    idx, gates, rank, cnt = _route(h, wt_hi, wt_lo, router_b.reshape(N_EXPERTS, 1),
                                   min(ROUTE_TOKENS, s_len))

    experts = jnp.arange(N_EXPERTS, dtype=I32)
    counts = cnt[:, 0].astype(I32)
    nblk = (counts + rb - 1) // rb
    blk_end = jnp.cumsum(nblk)
    blk_start = blk_end - nblk
    n_blocks = (TOP_K * s_len + rb - 1) // rb + N_EXPERTS
    n_used = blk_end[-1]
    blk_ids = jnp.minimum(jnp.arange(n_blocks, dtype=I32), n_used - 1)
    block_e = jnp.minimum(jnp.sum((blk_end[None, :] <= blk_ids[:, None]).astype(I32), axis=1), N_EXPERTS - 1)
    first = _lookup(blk_start, block_e) == blk_ids
    nonempty = (nblk > 0).astype(I32)
    visit = jnp.cumsum(nonempty) - 1
    later = jnp.logical_and(experts[None, :] > experts[:, None], nblk[None, :] > 0)
    next_e = jnp.min(jnp.where(later, experts[None, :], N_EXPERTS), axis=1)
    next_e = jnp.where(next_e < N_EXPERTS, next_e, -1)
    block_tab = jnp.stack([block_e, blk_ids, first.astype(I32), _lookup(visit % 2, block_e),
                           _lookup(next_e, block_e)]).astype(I32)
    last_blk = jnp.where(nblk > 0, blk_end - 1, -1).astype(I32)
    pos = _lookup(blk_start * rb, idx) + rank

    td = min(DISPATCH_TOKENS, s_len)
    xs = _dispatch(h, _token_blocks(pos, td), last_blk, n_blocks * rb, rb, td)
    yb = _experts(xs, block_tab, n_used.reshape(1).astype(I32), w1, b1, w2, b2, layer, rb)
    tc = min(COMBINE_TOKENS, s_len)
    return _combine(h, yb, _token_blocks(pos, tc), gates.T, p, ln_g, ln_b, wp, wg, tc)


def _rope_tables(positions):
    inv = 1.0 / (ROPE_THETA ** (jnp.arange(0, QK_ROPE_DIM, 2, dtype=F32) / QK_ROPE_DIM))
    ang = positions.astype(F32)[:, None] * inv
    cos, sin = jnp.cos(ang), jnp.sin(ang)
    zeros = jnp.zeros((positions.shape[0], LANES - QK_ROPE_DIM), F32)
    return (jnp.concatenate([cos, cos, zeros], axis=1), jnp.concatenate([-sin, sin, zeros], axis=1))


def _swap_halves(w):
    half = QK_ROPE_DIM // 2
    return jnp.concatenate([w[..., half:], w[..., :half]], axis=-1)


def kernel(x, p, positions, ln_mix_g, ln_mix_b, ln_ffn_g, ln_ffn_b, rg_w_in, rg_conv_w, rg_conv_b, rg_gate_a_w, rg_gate_a_b, rg_gate_x_w, rg_gate_x_b, rg_lambda, rg_w_out, mla_w_dq, mla_q_norm, mla_w_uq, mla_w_o, kv_w_dkv, kv_norm, kv_w_ukv, moe_router_w, moe_router_b, moe_w1, moe_b1, moe_w2, moe_b2, ple_w_proj, ple_w_gate):
    bsz, s_len, _ = x.shape
    assert bsz == 1
    h = x.reshape(s_len, D_MODEL)
    cosp, sinp = _rope_tables(positions.reshape(s_len))
    tm = min(512, s_len)
    k_heads = v_heads = None
    for i in range(DEPTH):
        if i < N_A_LAYERS:
            u = _linear(h, rg_w_in[i].astype(BF16), F32, min(1024, s_len), 1024)
            yg = _rglru(u, rg_conv_w[i], rg_conv_b[i].reshape(1, D_RNN),
                        rg_gate_a_w[i].astype(BF16), rg_gate_a_b[i].reshape(RG_HEADS, 1, RG_BLOCK),
                        rg_gate_x_w[i].astype(BF16), rg_gate_x_b[i].reshape(RG_HEADS, 1, RG_BLOCK),
                        rg_lambda[i].reshape(1, D_RNN), min(256, s_len))
            w_o = rg_w_out[i]
        else:
            j = i - N_A_LAYERS
            if k_heads is None:
                lat, rope = kv_w_dkv[:, :KV_LORA_RANK], kv_w_dkv[:, KV_LORA_RANK:]
                zc = jnp.zeros((D_MODEL, LANES - QK_ROPE_DIM), F32)
                wd = jnp.concatenate([lat, rope, zc, _swap_halves(rope), zc], axis=1).astype(BF16)
                k_heads, v_heads = _kv_proj(h, wd, kv_norm.reshape(1, -1), kv_w_ukv.astype(BF16),
                                            cosp, sinp, min(256, s_len))
            wq = mla_w_uq[j].reshape(Q_LORA_RANK, MLA_HEADS, QK_NOPE_DIM + QK_ROPE_DIM)
            nope, rope = wq[..., :QK_NOPE_DIM], wq[..., QK_NOPE_DIM:]
            zq = jnp.zeros((Q_LORA_RANK, MLA_HEADS, LANES - QK_ROPE_DIM), F32)
            wa = jnp.concatenate([nope, rope, zq], axis=-1).reshape(Q_LORA_RANK, MLA_HEADS * QK_PAD)
            wb = jnp.concatenate([_swap_halves(rope), zq], axis=-1).reshape(Q_LORA_RANK, MLA_HEADS * LANES)
            q_heads = _q_proj(h, mla_w_dq[j].astype(BF16), mla_q_norm[j].reshape(1, -1),
                              wa.astype(BF16), wb.astype(BF16), cosp, sinp, min(256, s_len))
            yg = _attention(q_heads, k_heads, v_heads, min(1024, s_len))
            w_o = mla_w_o[j]
        h = _proj_ln(yg, w_o.astype(BF16), h, ln_mix_g[i].reshape(1, -1), ln_mix_b[i].reshape(1, -1), tm)
        h = _moe_ln_ple(h, p[i].reshape(s_len, PLE_DIM), moe_router_w[i], moe_router_b[i],
                        moe_w1, moe_b1.reshape(DEPTH, N_EXPERTS, 1, -1),
                        moe_w2, moe_b2.reshape(DEPTH, N_EXPERTS, 1, -1), i,
                        ln_ffn_g[i].reshape(1, -1), ln_ffn_b[i].reshape(1, -1),
                        ple_w_proj[i].astype(BF16), ple_w_gate[i].astype(BF16))
    return h.reshape(bsz, s_len, D_MODEL)
```

```python
import functools

import jax
import jax.numpy as jnp
from jax import lax
from jax.experimental import pallas as pl
from jax.experimental.pallas import tpu as pltpu

F32 = jnp.float32
BF16 = jnp.bfloat16
I32 = jnp.int32
U32 = jnp.uint32

D_MODEL = 2048
DEPTH = 4
N_A_LAYERS = DEPTH // 2
DEEPNORM_ALPHA = (2 * DEPTH) ** 0.25
LN_EPS = 1e-5
RMS_EPS = 1e-6
D_RNN = D_MODEL
RG_HEADS = 8
RG_BLOCK = D_RNN // RG_HEADS
CONV_WIDTH = 4
RG_C = 8.0
MLA_HEADS = 16
Q_LORA_RANK = 768
KV_LORA_RANK = 512
QK_NOPE_DIM = 128
QK_ROPE_DIM = 64
V_HEAD_DIM = 128
ROPE_THETA = 10000.0
ATTN_SCALE = (QK_NOPE_DIM + QK_ROPE_DIM) ** -0.5
N_EXPERTS = 32
TOP_K = 4
D_EXPERT = 1024
SWIGLU_LIMIT = 7.0
SWIGLU_ALPHA = 1.702
PLE_DIM = 256

LANES = 128
SUBLANES = 8
QK_PAD = 2 * LANES
MASK_VALUE = -1e30
NEG_BIG = -3.0e38

VMEM_LIMIT = 56 * 1024 * 1024

NT_DIMS = (((1,), (1,)), ((), ()))


def _params(sem, vmem=VMEM_LIMIT):
    return pltpu.CompilerParams(dimension_semantics=sem, vmem_limit_bytes=vmem)


def _layer_norm(z, g, b):
    mu = jnp.mean(z, axis=-1, keepdims=True)
    zc = z - mu
    var = jnp.mean(zc * zc, axis=-1, keepdims=True)
    return zc * lax.rsqrt(var + LN_EPS) * g + b


def _rms_norm(x, g):
    return x * lax.rsqrt(jnp.mean(x * x, axis=-1, keepdims=True) + RMS_EPS) * g


def _linear_kernel(x_ref, w_ref, o_ref):
    x = x_ref[...].astype(BF16)
    o_ref[...] = jnp.dot(x, w_ref[...], preferred_element_type=F32).astype(o_ref.dtype)


def _linear(x, w, out_dtype, tm, tn):
    m, k = x.shape
    n = w.shape[1]
    return pl.pallas_call(
        _linear_kernel,
        grid=(m // tm, n // tn),
        in_specs=[pl.BlockSpec((tm, k), lambda i, j: (i, 0)),
                  pl.BlockSpec((k, tn), lambda i, j: (0, j))],
        out_specs=pl.BlockSpec((tm, tn), lambda i, j: (i, j)),
        out_shape=jax.ShapeDtypeStruct((m, n), out_dtype),
        compiler_params=_params(("parallel", "parallel")),
        name="linear",
    )(x, w)


def _proj_ln_kernel(x_ref, w_ref, h_ref, g_ref, b_ref, o_ref):
    mix = jnp.dot(x_ref[...].astype(BF16), w_ref[...], preferred_element_type=F32)
    z = DEEPNORM_ALPHA * h_ref[...] + mix
    o_ref[...] = _layer_norm(z, g_ref[...], b_ref[...])


def _proj_ln(x, w, h, g, b, tm):
    m, k = x.shape
    n = w.shape[1]
    return pl.pallas_call(
        _proj_ln_kernel,
        grid=(m // tm,),
        in_specs=[pl.BlockSpec((tm, k), lambda i: (i, 0)),
                  pl.BlockSpec((k, n), lambda i: (0, 0)),
                  pl.BlockSpec((tm, n), lambda i: (i, 0)),
                  pl.BlockSpec((1, n), lambda i: (0, 0)),
                  pl.BlockSpec((1, n), lambda i: (0, 0))],
        out_specs=pl.BlockSpec((tm, n), lambda i: (i, 0)),
        out_shape=jax.ShapeDtypeStruct((m, n), F32),
        compiler_params=_params(("parallel",)),
        name="proj_ln",
    )(x, w, h, g, b)


def _shift_rows(x, d, fill_rows):
    xr = pltpu.roll(x, d, 0)
    row8 = lax.broadcasted_iota(I32, fill_rows.shape, 0)
    top = jnp.where(row8 < d, fill_rows, xr[:SUBLANES])
    return jnp.concatenate([top, xr[SUBLANES:]], axis=0)


def _rglru_kernel(gate_ref, rec_ref, cw_ref, cb_ref, wa_ref, ba_ref, wx_ref, bx_ref, lam_ref,
                  o_ref, tail_ref, state_ref):
    t_blk = rec_ref.shape[0]

    @pl.when(pl.program_id(1) == 0)
    def _():
        tail_ref[...] = jnp.zeros_like(tail_ref)
        state_ref[...] = jnp.zeros_like(state_ref)

    x = rec_ref[...]
    prev = tail_ref[...]
    conv = cb_ref[...] + cw_ref[CONV_WIDTH - 1:CONV_WIDTH, :] * x
    for d in range(1, CONV_WIDTH):
        shifted = _shift_rows(x, d, pltpu.roll(prev, d, 0))
        conv = conv + cw_ref[CONV_WIDTH - 1 - d:CONV_WIDTH - d, :] * shifted
    tail_ref[...] = x[t_blk - SUBLANES:]

    xb = conv.astype(BF16)
    r = jax.nn.sigmoid(jnp.dot(xb, wa_ref[0], preferred_element_type=F32) + ba_ref[0])
    ig = jax.nn.sigmoid(jnp.dot(xb, wx_ref[0], preferred_element_type=F32) + bx_ref[0])
    lam = lam_ref[...]
    log_sig = jnp.minimum(lam, 0.0) - jnp.log1p(jnp.exp(-jnp.abs(lam)))
    log_a = RG_C * r * log_sig
    a = jnp.exp(log_a)
    th = jnp.tanh(log_a)
    b = jnp.sqrt(-2.0 * th / (1.0 - th)) * (ig * conv)

    ones8 = jnp.ones((SUBLANES, x.shape[1]), F32)
    zeros8 = jnp.zeros((SUBLANES, x.shape[1]), F32)
    s = 1
    while s < t_blk:
        if s < SUBLANES:
            a_sh = _shift_rows(a, s, ones8)
            b_sh = _shift_rows(b, s, zeros8)
        else:
            a_sh = jnp.concatenate([jnp.ones((s, x.shape[1]), F32), a[:t_blk - s]], axis=0)
            b_sh = jnp.concatenate([jnp.zeros((s, x.shape[1]), F32), b[:t_blk - s]], axis=0)
        b = a * b_sh + b
        a = a * a_sh
        s *= 2
    h = a * state_ref[SUBLANES - 1:SUBLANES, :] + b
    state_ref[...] = h[t_blk - SUBLANES:]

    o_ref[...] = (jax.nn.gelu(gate_ref[...], approximate=True) * h).astype(o_ref.dtype)


def _rglru(u, conv_w, conv_b, wa, ba, wx, bx, lam, t_blk):
    s_len = u.shape[0]
    c = RG_BLOCK
    return pl.pallas_call(
        _rglru_kernel,
        grid=(RG_HEADS, s_len // t_blk),
        in_specs=[pl.BlockSpec((t_blk, c), lambda h, t: (t, h)),
                  pl.BlockSpec((t_blk, c), lambda h, t: (t, RG_HEADS + h)),
                  pl.BlockSpec((CONV_WIDTH, c), lambda h, t: (0, h)),
                  pl.BlockSpec((1, c), lambda h, t: (0, h)),
                  pl.BlockSpec((1, c, c), lambda h, t: (h, 0, 0)),
                  pl.BlockSpec((1, 1, c), lambda h, t: (h, 0, 0)),
                  pl.BlockSpec((1, c, c), lambda h, t: (h, 0, 0)),
                  pl.BlockSpec((1, 1, c), lambda h, t: (h, 0, 0)),
                  pl.BlockSpec((1, c), lambda h, t: (0, h))],
        out_specs=pl.BlockSpec((t_blk, c), lambda h, t: (t, h)),
        out_shape=jax.ShapeDtypeStruct((s_len, D_RNN), BF16),
        scratch_shapes=[pltpu.VMEM((SUBLANES, c), F32), pltpu.VMEM((SUBLANES, c), F32)],
        compiler_params=_params(("parallel", "arbitrary")),
        name="rglru",
    )(u, u, conv_w, conv_b, wa, ba, wx, bx, lam)


def _q_proj_kernel(h_ref, wdq_ref, qn_ref, wa_ref, wb_ref, cos_ref, sin_ref, q_ref):
    cq = jnp.dot(h_ref[...].astype(BF16), wdq_ref[...], preferred_element_type=F32)
    cq = _rms_norm(cq, qn_ref[...]).astype(BF16)
    qa = jnp.dot(cq, wa_ref[...], preferred_element_type=F32)
    qb = jnp.dot(cq, wb_ref[...], preferred_element_type=F32)
    cosp = cos_ref[...]
    sinp = sin_ref[...]
    for hd in range(MLA_HEADS):
        nope = qa[:, hd * QK_PAD:hd * QK_PAD + LANES]
        rope = qa[:, hd * QK_PAD + LANES:(hd + 1) * QK_PAD] * cosp + qb[:, hd * LANES:(hd + 1) * LANES] * sinp
        q_ref[hd, :, 0:LANES] = (nope * ATTN_SCALE).astype(BF16)
        q_ref[hd, :, LANES:QK_PAD] = (rope * ATTN_SCALE).astype(BF16)


def _q_proj(h, wdq, qn, wa, wb, cosp, sinp, tm):
    s_len = h.shape[0]
    full = lambda i: (0, 0)
    return pl.pallas_call(
        _q_proj_kernel,
        grid=(s_len // tm,),
        in_specs=[pl.BlockSpec((tm, D_MODEL), lambda i: (i, 0)),
                  pl.BlockSpec(wdq.shape, full),
                  pl.BlockSpec(qn.shape, full),
                  pl.BlockSpec(wa.shape, full),
                  pl.BlockSpec(wb.shape, full),
                  pl.BlockSpec((tm, LANES), lambda i: (i, 0)),
                  pl.BlockSpec((tm, LANES), lambda i: (i, 0))],
        out_specs=pl.BlockSpec((MLA_HEADS, tm, QK_PAD), lambda i: (0, i, 0)),
        out_shape=jax.ShapeDtypeStruct((MLA_HEADS, s_len, QK_PAD), BF16),
        compiler_params=_params(("parallel",)),
        name="q_proj",
    )(h, wdq, qn, wa, wb, cosp, sinp)


def _kv_proj_kernel(h_ref, wd_ref, kn_ref, wu_ref, cos_ref, sin_ref, k_ref, v_ref):
    ckv = jnp.dot(h_ref[...].astype(BF16), wd_ref[...], preferred_element_type=F32)
    c = _rms_norm(ckv[:, :KV_LORA_RANK], kn_ref[...]).astype(BF16)
    k_pe = (ckv[:, KV_LORA_RANK:KV_LORA_RANK + LANES] * cos_ref[...]
            + ckv[:, KV_LORA_RANK + LANES:KV_LORA_RANK + 2 * LANES] * sin_ref[...]).astype(BF16)
    kv = jnp.dot(c, wu_ref[...], preferred_element_type=F32)
    w = QK_NOPE_DIM + V_HEAD_DIM
    for hd in range(MLA_HEADS):
        k_ref[hd, :, 0:LANES] = kv[:, hd * w:hd * w + QK_NOPE_DIM].astype(BF16)
        k_ref[hd, :, LANES:QK_PAD] = k_pe
        v_ref[hd] = kv[:, hd * w + QK_NOPE_DIM:(hd + 1) * w].astype(BF16)


def _kv_proj(h, wd, kn, wu, cosp, sinp, tm):
    s_len = h.shape[0]
    full = lambda i: (0, 0)
    return pl.pallas_call(
        _kv_proj_kernel,
        grid=(s_len // tm,),
        in_specs=[pl.BlockSpec((tm, D_MODEL), lambda i: (i, 0)),
                  pl.BlockSpec(wd.shape, full),
                  pl.BlockSpec(kn.shape, full),
                  pl.BlockSpec(wu.shape, full),
                  pl.BlockSpec((tm, LANES), lambda i: (i, 0)),
                  pl.BlockSpec((tm, LANES), lambda i: (i, 0))],
        out_specs=[pl.BlockSpec((MLA_HEADS, tm, QK_PAD), lambda i: (0, i, 0)),
                   pl.BlockSpec((MLA_HEADS, tm, V_HEAD_DIM), lambda i: (0, i, 0))],
        out_shape=[jax.ShapeDtypeStruct((MLA_HEADS, s_len, QK_PAD), BF16),
                   jax.ShapeDtypeStruct((MLA_HEADS, s_len, V_HEAD_DIM), BF16)],
        compiler_params=_params(("parallel",)),
        name="kv_proj",
    )(h, wd, kn, wu, cosp, sinp)


N_SUB = 4


def _attn_kernel(q_ref, k_ref, v_ref, o_ref, vaug_ref, m_ref, acc_ref):
    t = q_ref.shape[1] // N_SUB
    i = pl.program_id(1)

    @pl.when(i == 0)
    def _():
        vaug_ref[:, 0:V_HEAD_DIM] = v_ref[0]
        vaug_ref[:, V_HEAD_DIM:] = jnp.ones((vaug_ref.shape[0], LANES), BF16)

    m_ref[...] = jnp.full(m_ref.shape, MASK_VALUE, F32)
    acc_ref[...] = jnp.zeros_like(acc_ref)

    def load_chunk(j):
        start = pl.multiple_of(j * t, t)
        return k_ref[0, pl.ds(start, t), :], vaug_ref[pl.ds(start, t), :]

    def update(sub, kc, vc, masked):
        q = q_ref[0, sub * t:(sub + 1) * t, :]
        s = lax.dot_general(q, kc, NT_DIMS, preferred_element_type=F32)
        if masked:
            row = lax.broadcasted_iota(I32, s.shape, 0)
            col = lax.broadcasted_iota(I32, s.shape, 1)
            s = jnp.where(col <= row, s, MASK_VALUE)
        m_prev = m_ref[sub]
        m_new = jnp.maximum(m_prev, jnp.max(s, axis=1, keepdims=True))
        alpha = jnp.exp(m_prev - m_new)
        p = jnp.exp(s - jnp.concatenate([m_new] * (t // LANES), axis=1))
        pv = jnp.dot(p.astype(BF16), vc, preferred_element_type=F32)
        acc_ref[sub] = acc_ref[sub] * jnp.concatenate([alpha, alpha], axis=1) + pv
        m_ref[sub] = m_new

    def body(jj, carry):
        for c in range(N_SUB):
            kc, vc = load_chunk(jj * N_SUB + c)
            for sub in range(N_SUB):
                update(sub, kc, vc, False)
        return carry

    lax.fori_loop(0, i, body, 0)
    for c in range(N_SUB):
        kc, vc = load_chunk(i * N_SUB + c)
        for sub in range(c, N_SUB):
            update(sub, kc, vc, sub == c)
    for sub in range(N_SUB):
        acc = acc_ref[sub]
        o_ref[sub * t:(sub + 1) * t, :] = (acc[:, 0:V_HEAD_DIM] / acc[:, V_HEAD_DIM:]).astype(o_ref.dtype)


def _attention(q, k, v, tq):
    s_len = q.shape[1]
    t = tq // N_SUB
    return pl.pallas_call(
        _attn_kernel,
        grid=(MLA_HEADS, s_len // tq),
        in_specs=[pl.BlockSpec((1, tq, QK_PAD), lambda h, i: (h, i, 0)),
                  pl.BlockSpec((1, s_len, QK_PAD), lambda h, i: (h, 0, 0)),
                  pl.BlockSpec((1, s_len, V_HEAD_DIM), lambda h, i: (h, 0, 0))],
        out_specs=pl.BlockSpec((tq, V_HEAD_DIM), lambda h, i: (i, h)),
        out_shape=jax.ShapeDtypeStruct((s_len, MLA_HEADS * V_HEAD_DIM), BF16),
        scratch_shapes=[pltpu.VMEM((s_len, 2 * LANES), BF16),
                        pltpu.VMEM((N_SUB, t, LANES), F32),
                        pltpu.VMEM((N_SUB, t, 2 * LANES), F32)],
        compiler_params=_params(("parallel", "arbitrary")),
        name="attention",
    )(q, k, v)


def _route_kernel(x_ref, wh_ref, wl_ref, b_ref, idx_ref, gate_ref, rank_ref, cnt_ref, carry_ref):
    tb = x_ref.shape[0]

    @pl.when(pl.program_id(0) == 0)
    def _():
        carry_ref[...] = jnp.zeros_like(carry_ref)

    x = x_ref[...]
    xh = x.astype(BF16)
    xl = (x - xh.astype(F32)).astype(BF16)
    wh = wh_ref[...]
    logits = (lax.dot_general(wh, xh, NT_DIMS, preferred_element_type=F32)
              + lax.dot_general(wh, xl, NT_DIMS, preferred_element_type=F32)
              + lax.dot_general(wl_ref[...], xh, NT_DIMS, preferred_element_type=F32)
              + b_ref[...])
    erow = lax.broadcasted_iota(I32, logits.shape, 0).astype(F32)
    vals, sels = [], []
    rem = logits
    for _ in range(TOP_K):
        mx = jnp.max(rem, axis=0, keepdims=True)
        ix = jnp.min(jnp.where(rem == mx, erow, float(N_EXPERTS)), axis=0, keepdims=True)
        sel = erow == ix
        rem = jnp.where(sel, NEG_BIG, rem)
        vals.append(mx)
        sels.append(sel)
        idx_ref[len(vals) - 1:len(vals), :] = ix.astype(I32)
    es = [jnp.exp(v - vals[0]) for v in vals]
    den = es[0] + es[1] + es[2] + es[3]
    onehot = jnp.zeros(logits.shape, F32)
    for k in range(TOP_K):
        gate_ref[k:k + 1, :] = es[k] / den
        onehot = onehot + sels[k].astype(F32)
    before = lax.broadcasted_iota(I32, (tb, tb), 0) < lax.broadcasted_iota(I32, (tb, tb), 1)
    excl = jnp.dot(onehot.astype(BF16), before.astype(BF16), preferred_element_type=F32)
    excl = excl + carry_ref[:, 0:1]
    for k in range(TOP_K):
        rank_ref[k:k + 1, :] = jnp.sum(jnp.where(sels[k], excl, 0.0), axis=0, keepdims=True).astype(I32)
    carry_ref[...] = carry_ref[...] + jnp.sum(onehot, axis=1, keepdims=True)
    cnt_ref[...] = carry_ref[...]


def _route(h, wt_hi, wt_lo, bias, tb):
    s_len = h.shape[0]
    full = lambda i: (0, 0)
    tok = pl.BlockSpec((TOP_K, tb), lambda i: (0, i))
    return pl.pallas_call(
        _route_kernel,
        grid=(s_len // tb,),
        in_specs=[pl.BlockSpec((tb, D_MODEL), lambda i: (i, 0)),
                  pl.BlockSpec(wt_hi.shape, full),
                  pl.BlockSpec(wt_lo.shape, full),
                  pl.BlockSpec(bias.shape, full)],
        out_specs=[tok, tok, tok, pl.BlockSpec((N_EXPERTS, LANES), full)],
        out_shape=[jax.ShapeDtypeStruct((TOP_K, s_len), I32),
                   jax.ShapeDtypeStruct((TOP_K, s_len), F32),
                   jax.ShapeDtypeStruct((TOP_K, s_len), I32),
                   jax.ShapeDtypeStruct((N_EXPERTS, LANES), F32)],
        scratch_shapes=[pltpu.VMEM((N_EXPERTS, LANES), F32)],
        compiler_params=_params(("arbitrary",)),
        name="route",
    )(h, wt_hi, wt_lo, bias)


def _pack_rows(x):
    half = x.shape[1] // 2
    lo = lax.bitcast_convert_type(x[:, :half].astype(BF16).astype(F32), U32)
    hi = lax.bitcast_convert_type(x[:, half:].astype(BF16).astype(F32), U32)
    return (hi & jnp.uint32(0xFFFF0000)) | (lo >> 16)


def _unpack_rows(w, dtype):
    lo = lax.bitcast_convert_type(w << 16, F32)
    hi = lax.bitcast_convert_type(w & jnp.uint32(0xFFFF0000), F32)
    return jnp.concatenate([lo, hi], axis=1).astype(dtype)


def _dispatch_kernel(last_ref, pos_ref, x_ref, xs_ref, buf_ref, zero_ref, sem_ref, zsem_ref, *, rb):
    i = pl.program_id(0)
    n_steps = pl.num_programs(0)
    td = x_ref.shape[0]
    n_rows = TOP_K * td
    slot = i % 2

    def row_copy(sl, t, dst_row):
        return pltpu.make_async_copy(buf_ref.at[sl, pl.ds(t, 1)], xs_ref.at[pl.ds(dst_row, 1)],
                                     sem_ref.at[sl])

    def wait_rows(sl):
        for _ in range(n_rows):
            row_copy(sl, 0, 0).wait()

    @pl.when(i == 0)
    def _():
        zero_ref[...] = jnp.zeros_like(zero_ref)

        def zero_copy(e):
            return pltpu.make_async_copy(zero_ref, xs_ref.at[pl.ds(last_ref[e] * rb, rb)], zsem_ref)

        for e in range(N_EXPERTS):
            @pl.when(last_ref[e] >= 0)
            def _():
                zero_copy(e).start()
        for e in range(N_EXPERTS):
            @pl.when(last_ref[e] >= 0)
            def _():
                zero_copy(e).wait()

    @pl.when(i >= 2)
    def _():
        wait_rows(slot)

    buf_ref[slot] = _pack_rows(x_ref[...])
    for t in range(td):
        for k in range(TOP_K):
            row_copy(slot, t, pos_ref[0, 0, k * td + t]).start()

    @pl.when(i == n_steps - 1)
    def _():
        wait_rows(slot)

        @pl.when(n_steps >= 2)
        def _():
            wait_rows(1 - slot)


def _dispatch(h, pos_blocks, last_blk, n_rows_total, rb, td):
    s_len = h.shape[0]
    half = D_MODEL // 2
    grid_spec = pltpu.PrefetchScalarGridSpec(
        num_scalar_prefetch=1,
        grid=(s_len // td,),
        in_specs=[pl.BlockSpec((1, 1, TOP_K * td), lambda i, last: (i, 0, 0), memory_space=pltpu.SMEM),
                  pl.BlockSpec((td, D_MODEL), lambda i, last: (i, 0))],
        out_specs=pl.BlockSpec(memory_space=pl.ANY),
        scratch_shapes=[pltpu.VMEM((2, td, half), U32),
                        pltpu.VMEM((rb, half), U32),
                        pltpu.SemaphoreType.DMA((2,)),
                        pltpu.SemaphoreType.DMA(())],
    )
    return pl.pallas_call(
        functools.partial(_dispatch_kernel, rb=rb),
        grid_spec=grid_spec,
        out_shape=jax.ShapeDtypeStruct((n_rows_total, half), U32),
        compiler_params=_params(("arbitrary",)),
        name="dispatch",
    )(last_blk, pos_blocks, h)


WEIGHT_CAST_ROWS = 256


def _expert_matmul(tab_ref, b, x, w_hbm, stage_ref, wbf_ref, sem_ref, layer, finish):
    def copy(e, sl):
        return pltpu.make_async_copy(w_hbm.at[layer, e], stage_ref.at[sl], sem_ref.at[sl])

    first = tab_ref[2, b] == 1

    @pl.when(first)
    def _():
        e = tab_ref[0, b]
        sl = tab_ref[3, b]
        nxt = tab_ref[4, b]

        @pl.when(b == 0)
        def _():
            copy(e, sl).start()

        @pl.when(nxt >= 0)
        def _():
            copy(nxt, 1 - sl).start()

        copy(e, sl).wait()
        acc = None
        for c in range(wbf_ref.shape[0] // WEIGHT_CAST_ROWS):
            rows = slice(c * WEIGHT_CAST_ROWS, (c + 1) * WEIGHT_CAST_ROWS)
            wc = stage_ref[sl, rows, :].astype(BF16)
            wbf_ref[rows, :] = wc
            part = jnp.dot(x[:, rows], wc, preferred_element_type=F32)
            acc = part if acc is None else acc + part
        finish(acc)

    @pl.when(jnp.logical_not(first))
    def _():
        finish(jnp.dot(x, wbf_ref[...], preferred_element_type=F32))


def _up_kernel(tab_ref, nused_ref, xs_ref, b1_ref, w1_hbm, act_ref, stage_ref, w1b_ref, sem_ref, *, layer):
    b = pl.program_id(0)

    def finish(hh):
        hh = hh + b1_ref[0, 0]
        glu = jnp.minimum(hh[:, :D_EXPERT], SWIGLU_LIMIT)
        lin = jnp.clip(hh[:, D_EXPERT:], -SWIGLU_LIMIT, SWIGLU_LIMIT)
        act = glu * jax.nn.sigmoid(SWIGLU_ALPHA * glu) * (lin + 1.0)
        act_ref[...] = act.astype(act_ref.dtype)

    @pl.when(b < nused_ref[0])
    def _():
        _expert_matmul(tab_ref, b, _unpack_rows(xs_ref[...], BF16), w1_hbm, stage_ref, w1b_ref, sem_ref,
                       layer, finish)


def _down_kernel(tab_ref, nused_ref, act_ref, b2_ref, w2_hbm, y_ref, stage_ref, w2b_ref, sem_ref, *, layer):
    b = pl.program_id(0)

    def finish(y):
        y_ref[...] = _pack_rows(y + b2_ref[0, 0])

    @pl.when(b < nused_ref[0])
    def _():
        _expert_matmul(tab_ref, b, act_ref[...], w2_hbm, stage_ref, w2b_ref, sem_ref, layer, finish)


def _experts(xs, block_tab, n_used, w1, b1, w2, b2, layer, rb):
    n_rows = xs.shape[0]
    nb = n_rows // rb
    half = D_MODEL // 2
    rows = lambda b, tab, nu: (tab[1, b], 0)
    bsel = lambda b, tab, nu: (layer, tab[0, b], 0, 0)
    act = pl.pallas_call(
        functools.partial(_up_kernel, layer=layer),
        grid_spec=pltpu.PrefetchScalarGridSpec(
            num_scalar_prefetch=2, grid=(nb,),
            in_specs=[pl.BlockSpec((rb, half), rows),
                      pl.BlockSpec((1, 1, 1, 2 * D_EXPERT), bsel),
                      pl.BlockSpec(memory_space=pl.ANY)],
            out_specs=pl.BlockSpec((rb, D_EXPERT), rows),
            scratch_shapes=[pltpu.VMEM((2, D_MODEL, 2 * D_EXPERT), F32),
                            pltpu.VMEM((D_MODEL, 2 * D_EXPERT), BF16),
                            pltpu.SemaphoreType.DMA((2,))]),
        out_shape=jax.ShapeDtypeStruct((n_rows, D_EXPERT), BF16),
        compiler_params=_params(("arbitrary",)),
        name="moe_up",
    )(block_tab, n_used, xs, b1, w1)
    return pl.pallas_call(
        functools.partial(_down_kernel, layer=layer),
        grid_spec=pltpu.PrefetchScalarGridSpec(
            num_scalar_prefetch=2, grid=(nb,),
            in_specs=[pl.BlockSpec((rb, D_EXPERT), rows),
                      pl.BlockSpec((1, 1, 1, D_MODEL), bsel),
                      pl.BlockSpec(memory_space=pl.ANY)],
            out_specs=pl.BlockSpec((rb, half), rows),
            scratch_shapes=[pltpu.VMEM((2, D_EXPERT, D_MODEL), F32),
                            pltpu.VMEM((D_EXPERT, D_MODEL), BF16),
                            pltpu.SemaphoreType.DMA((2,))]),
        out_shape=jax.ShapeDtypeStruct((n_rows, half), U32),
        compiler_params=_params(("arbitrary",)),
        name="moe_down",
    )(block_tab, n_used, act, b2, w2)


def _combine_kernel(pos_ref, posn_ref, h_ref, gate_ref, p_ref, g_ref, b_ref, wp_ref, wg_ref, yb_ref,
                    o_ref, rows_ref, sem_ref):
    i = pl.program_id(0)
    n_steps = pl.num_programs(0)
    tc = h_ref.shape[0]
    slot = i % 2

    def row_copy(sl, k, t, src_row):
        return pltpu.make_async_copy(yb_ref.at[pl.ds(src_row, 1)], rows_ref.at[sl, k, pl.ds(t, 1)],
                                     sem_ref.at[sl])

    def fetch(sl, p_ref_):
        for t in range(tc):
            for k in range(TOP_K):
                row_copy(sl, k, t, p_ref_[0, 0, k * tc + t]).start()

    @pl.when(i == 0)
    def _():
        fetch(slot, pos_ref)

    @pl.when(i + 1 < n_steps)
    def _():
        fetch(1 - slot, posn_ref)

    for _ in range(TOP_K * tc):
        row_copy(slot, 0, 0, 0).wait()

    gates = gate_ref[...]
    ffn = gates[:, 0:1] * _unpack_rows(rows_ref[slot, 0], F32)
    for k in range(1, TOP_K):
        ffn = ffn + gates[:, k:k + 1] * _unpack_rows(rows_ref[slot, k], F32)
    h2 = _layer_norm(DEEPNORM_ALPHA * h_ref[...] + ffn, g_ref[...], b_ref[...])
    emb = jnp.dot(p_ref[...].astype(BF16), wp_ref[...], preferred_element_type=F32)
    gate = jax.nn.sigmoid(jnp.dot(h2.astype(BF16), wg_ref[...], preferred_element_type=F32))
    o_ref[...] = h2 + emb * gate


def _combine(h, yb, pos_blocks, gates_t, p, g, b, wp, wg, tc):
    s_len = h.shape[0]
    n_steps = s_len // tc
    full = lambda i: (0, 0)
    smem_blk = (1, 1, TOP_K * tc)
    return pl.pallas_call(
        _combine_kernel,
        grid=(n_steps,),
        in_specs=[pl.BlockSpec(smem_blk, lambda i: (i, 0, 0), memory_space=pltpu.SMEM),
                  pl.BlockSpec(smem_blk, lambda i: (jnp.minimum(i + 1, n_steps - 1), 0, 0),
                               memory_space=pltpu.SMEM),
                  pl.BlockSpec((tc, D_MODEL), lambda i: (i, 0)),
                  pl.BlockSpec((tc, TOP_K), lambda i: (i, 0)),
                  pl.BlockSpec((tc, PLE_DIM), lambda i: (i, 0)),
                  pl.BlockSpec((1, D_MODEL), full),
                  pl.BlockSpec((1, D_MODEL), full),
                  pl.BlockSpec(wp.shape, full),
                  pl.BlockSpec(wg.shape, full),
                  pl.BlockSpec(memory_space=pl.ANY)],
        out_specs=pl.BlockSpec((tc, D_MODEL), lambda i: (i, 0)),
        out_shape=jax.ShapeDtypeStruct((s_len, D_MODEL), F32),
        scratch_shapes=[pltpu.VMEM((2, TOP_K, tc, D_MODEL // 2), U32),
                        pltpu.SemaphoreType.DMA((2,))],
        compiler_params=_params(("arbitrary",)),
        name="moe_combine",
    )(pos_blocks, pos_blocks, h, gates_t, p, g, b, wp, wg, yb)


def _token_blocks(a, tb):
    s_len = a.shape[1]
    return a.reshape(TOP_K, s_len // tb, tb).transpose(1, 0, 2).reshape(s_len // tb, 1, TOP_K * tb)


ROW_BLOCK = 256
DISPATCH_TOKENS = 128
COMBINE_TOKENS = 256
ROUTE_TOKENS = 512


def _lookup(table, keys):
    hit = keys[..., None] == jnp.arange(table.shape[0], dtype=I32)
    return jnp.sum(jnp.where(hit, table, 0), axis=-1)


def _moe_ln_ple(h, p, router_w, router_b, w1, b1, w2, b2, layer, ln_g, ln_b, wp, wg):
    s_len = h.shape[0]
    rb = ROW_BLOCK
    wt = router_w.T
    wt_hi = wt.astype(BF16)
    wt_lo = (wt - wt_hi.astype(F32)).astype(BF16)
    idx, gates, rank, cnt = _route(h, wt_hi, wt_lo, router_b.reshape(N_EXPERTS, 1),
                                   min(ROUTE_TOKENS, s_len))

    experts = jnp.arange(N_EXPERTS, dtype=I32)
    counts = cnt[:, 0].astype(I32)
    nblk = (counts + rb - 1) // rb
    blk_end = jnp.cumsum(nblk)
    blk_start = blk_end - nblk
    n_blocks = (TOP_K * s_len + rb - 1) // rb + N_EXPERTS
    n_used = blk_end[-1]
    blk_ids = jnp.minimum(jnp.arange(n_blocks, dtype=I32), n_used - 1)
    block_e = jnp.minimum(jnp.sum((blk_end[None, :] <= blk_ids[:, None]).astype(I32), axis=1), N_EXPERTS - 1)
    first = _lookup(blk_start, block_e) == blk_ids
    nonempty = (nblk > 0).astype(I32)
    visit = jnp.cumsum(nonempty) - 1
    later = jnp.logical_and(experts[None, :] > experts[:, None], nblk[None, :] > 0)
    next_e = jnp.min(jnp.where(later, experts[None, :], N_EXPERTS), axis=1)
    next_e = jnp.where(next_e < N_EXPERTS, next_e, -1)
    block_tab = jnp.stack([block_e, blk_ids, first.astype(I32), _lookup(visit % 2, block_e),
                           _lookup(next_e, block_e)]).astype(I32)
    last_blk = jnp.where(nblk > 0, blk_end - 1, -1).astype(I32)
    pos = _lookup(blk_start * rb, idx) + rank

    td = min(DISPATCH_TOKENS, s_len)
    xs = _dispatch(h, _token_blocks(pos, td), last_blk, n_blocks * rb, rb, td)
    yb = _experts(xs, block_tab, n_used.reshape(1).astype(I32), w1, b1, w2, b2, layer, rb)
    tc = min(COMBINE_TOKENS, s_len)
    return _combine(h, yb, _token_blocks(pos, tc), gates.T, p, ln_g, ln_b, wp, wg, tc)


def _rope_tables(positions):
    inv = 1.0 / (ROPE_THETA ** (jnp.arange(0, QK_ROPE_DIM, 2, dtype=F32) / QK_ROPE_DIM))
    ang = positions.astype(F32)[:, None] * inv
    cos, sin = jnp.cos(ang), jnp.sin(ang)
    zeros = jnp.zeros((positions.shape[0], LANES - QK_ROPE_DIM), F32)
    return (jnp.concatenate([cos, cos, zeros], axis=1), jnp.concatenate([-sin, sin, zeros], axis=1))


def _swap_halves(w):
    half = QK_ROPE_DIM // 2
    return jnp.concatenate([w[..., half:], w[..., :half]], axis=-1)


def kernel(x, p, positions, ln_mix_g, ln_mix_b, ln_ffn_g, ln_ffn_b, rg_w_in, rg_conv_w, rg_conv_b, rg_gate_a_w, rg_gate_a_b, rg_gate_x_w, rg_gate_x_b, rg_lambda, rg_w_out, mla_w_dq, mla_q_norm, mla_w_uq, mla_w_o, kv_w_dkv, kv_norm, kv_w_ukv, moe_router_w, moe_router_b, moe_w1, moe_b1, moe_w2, moe_b2, ple_w_proj, ple_w_gate):
    bsz, s_len, _ = x.shape
    assert bsz == 1
    h = x.reshape(s_len, D_MODEL)
    cosp, sinp = _rope_tables(positions.reshape(s_len))
    tm = min(512, s_len)
    k_heads = v_heads = None
    for i in range(DEPTH):
        if i < N_A_LAYERS:
            u = _linear(h, rg_w_in[i].astype(BF16), F32, min(1024, s_len), 1024)
            yg = _rglru(u, rg_conv_w[i], rg_conv_b[i].reshape(1, D_RNN),
                        rg_gate_a_w[i].astype(BF16), rg_gate_a_b[i].reshape(RG_HEADS, 1, RG_BLOCK),
                        rg_gate_x_w[i].astype(BF16), rg_gate_x_b[i].reshape(RG_HEADS, 1, RG_BLOCK),
                        rg_lambda[i].reshape(1, D_RNN), min(256, s_len))
            w_o = rg_w_out[i]
        else:
            j = i - N_A_LAYERS
            if k_heads is None:
                lat, rope = kv_w_dkv[:, :KV_LORA_RANK], kv_w_dkv[:, KV_LORA_RANK:]
                zc = jnp.zeros((D_MODEL, LANES - QK_ROPE_DIM), F32)
                wd = jnp.concatenate([lat, rope, zc, _swap_halves(rope), zc], axis=1).astype(BF16)
                k_heads, v_heads = _kv_proj(h, wd, kv_norm.reshape(1, -1), kv_w_ukv.astype(BF16),
                                            cosp, sinp, min(256, s_len))
            wq = mla_w_uq[j].reshape(Q_LORA_RANK, MLA_HEADS, QK_NOPE_DIM + QK_ROPE_DIM)
            nope, rope = wq[..., :QK_NOPE_DIM], wq[..., QK_NOPE_DIM:]
            zq = jnp.zeros((Q_LORA_RANK, MLA_HEADS, LANES - QK_ROPE_DIM), F32)
            wa = jnp.concatenate([nope, rope, zq], axis=-1).reshape(Q_LORA_RANK, MLA_HEADS * QK_PAD)
            wb = jnp.concatenate([_swap_halves(rope), zq], axis=-1).reshape(Q_LORA_RANK, MLA_HEADS * LANES)
            q_heads = _q_proj(h, mla_w_dq[j].astype(BF16), mla_q_norm[j].reshape(1, -1),
                              wa.astype(BF16), wb.astype(BF16), cosp, sinp, min(256, s_len))
            yg = _attention(q_heads, k_heads, v_heads, min(2048, s_len))
            w_o = mla_w_o[j]
        h = _proj_ln(yg, w_o.astype(BF16), h, ln_mix_g[i].reshape(1, -1), ln_mix_b[i].reshape(1, -1), tm)
        h = _moe_ln_ple(h, p[i].reshape(s_len, PLE_DIM), moe_router_w[i], moe_router_b[i],
                        moe_w1, moe_b1.reshape(DEPTH, N_EXPERTS, 1, -1),
                        moe_w2, moe_b2.reshape(DEPTH, N_EXPERTS, 1, -1), i,
                        ln_ffn_g[i].reshape(1, -1), ln_ffn_b[i].reshape(1, -1),
                        ple_w_proj[i].astype(BF16), ple_w_gate[i].astype(BF16))
    return h.reshape(bsz, s_len, D_MODEL)
```

```python
import functools

import jax
import jax.numpy as jnp
from jax import lax
from jax.experimental import pallas as pl
from jax.experimental.pallas import tpu as pltpu

F32 = jnp.float32
BF16 = jnp.bfloat16
I32 = jnp.int32
U32 = jnp.uint32

D_MODEL = 2048
DEPTH = 4
N_A_LAYERS = DEPTH // 2
DEEPNORM_ALPHA = (2 * DEPTH) ** 0.25
LN_EPS = 1e-5
RMS_EPS = 1e-6
D_RNN = D_MODEL
RG_HEADS = 8
RG_BLOCK = D_RNN // RG_HEADS
CONV_WIDTH = 4
RG_C = 8.0
MLA_HEADS = 16
Q_LORA_RANK = 768
KV_LORA_RANK = 512
QK_NOPE_DIM = 128
QK_ROPE_DIM = 64
V_HEAD_DIM = 128
ROPE_THETA = 10000.0
ATTN_SCALE = (QK_NOPE_DIM + QK_ROPE_DIM) ** -0.5
N_EXPERTS = 32
TOP_K = 4
D_EXPERT = 1024
SWIGLU_LIMIT = 7.0
SWIGLU_ALPHA = 1.702
PLE_DIM = 256

LANES = 128
SUBLANES = 8
QK_PAD = 2 * LANES
MASK_VALUE = -1e30
NEG_BIG = -3.0e38

VMEM_LIMIT = 56 * 1024 * 1024

NT_DIMS = (((1,), (1,)), ((), ()))


def _params(sem, vmem=VMEM_LIMIT):
    return pltpu.CompilerParams(dimension_semantics=sem, vmem_limit_bytes=vmem)


def _layer_norm(z, g, b):
    mu = jnp.mean(z, axis=-1, keepdims=True)
    zc = z - mu
    var = jnp.mean(zc * zc, axis=-1, keepdims=True)
    return zc * lax.rsqrt(var + LN_EPS) * g + b


def _rms_norm(x, g):
    return x * lax.rsqrt(jnp.mean(x * x, axis=-1, keepdims=True) + RMS_EPS) * g


def _sigmoid(x):
    return 0.5 * (jnp.tanh(0.5 * x) + 1.0)


def _linear_kernel(x_ref, w_ref, o_ref):
    x = x_ref[...].astype(BF16)
    o_ref[...] = jnp.dot(x, w_ref[...], preferred_element_type=F32).astype(o_ref.dtype)


def _linear(x, w, out_dtype, tm, tn):
    m, k = x.shape
    n = w.shape[1]
    return pl.pallas_call(
        _linear_kernel,
        grid=(m // tm, n // tn),
        in_specs=[pl.BlockSpec((tm, k), lambda i, j: (i, 0)),
                  pl.BlockSpec((k, tn), lambda i, j: (0, j))],
        out_specs=pl.BlockSpec((tm, tn), lambda i, j: (i, j)),
        out_shape=jax.ShapeDtypeStruct((m, n), out_dtype),
        compiler_params=_params(("parallel", "parallel")),
        name="linear",
    )(x, w)


def _proj_ln_kernel(x_ref, w_ref, h_ref, g_ref, b_ref, o_ref):
    mix = jnp.dot(x_ref[...].astype(BF16), w_ref[...], preferred_element_type=F32)
    z = DEEPNORM_ALPHA * h_ref[...] + mix
    o_ref[...] = _layer_norm(z, g_ref[...], b_ref[...])


def _proj_ln(x, w, h, g, b, tm):
    m, k = x.shape
    n = w.shape[1]
    return pl.pallas_call(
        _proj_ln_kernel,
        grid=(m // tm,),
        in_specs=[pl.BlockSpec((tm, k), lambda i: (i, 0)),
                  pl.BlockSpec((k, n), lambda i: (0, 0)),
                  pl.BlockSpec((tm, n), lambda i: (i, 0)),
                  pl.BlockSpec((1, n), lambda i: (0, 0)),
                  pl.BlockSpec((1, n), lambda i: (0, 0))],
        out_specs=pl.BlockSpec((tm, n), lambda i: (i, 0)),
        out_shape=jax.ShapeDtypeStruct((m, n), F32),
        compiler_params=_params(("parallel",)),
        name="proj_ln",
    )(x, w, h, g, b)


def _shift_rows(x, d, fill_rows):
    xr = pltpu.roll(x, d, 0)
    row8 = lax.broadcasted_iota(I32, fill_rows.shape, 0)
    top = jnp.where(row8 < d, fill_rows, xr[:SUBLANES])
    return jnp.concatenate([top, xr[SUBLANES:]], axis=0)


def _rglru_kernel(gate_ref, rec_ref, cw_ref, cb_ref, wa_ref, ba_ref, wx_ref, bx_ref, lam_ref,
                  o_ref, tail_ref, state_ref):
    t_blk = rec_ref.shape[0]

    @pl.when(pl.program_id(1) == 0)
    def _():
        tail_ref[...] = jnp.zeros_like(tail_ref)
        state_ref[...] = jnp.zeros_like(state_ref)

    x = rec_ref[...]
    prev = tail_ref[...]
    conv = cb_ref[...] + cw_ref[CONV_WIDTH - 1:CONV_WIDTH, :] * x
    for d in range(1, CONV_WIDTH):
        shifted = _shift_rows(x, d, pltpu.roll(prev, d, 0))
        conv = conv + cw_ref[CONV_WIDTH - 1 - d:CONV_WIDTH - d, :] * shifted
    tail_ref[...] = x[t_blk - SUBLANES:]

    xb = conv.astype(BF16)
    r = _sigmoid(jnp.dot(xb, wa_ref[0], preferred_element_type=F32) + ba_ref[0])
    ig = _sigmoid(jnp.dot(xb, wx_ref[0], preferred_element_type=F32) + bx_ref[0])
    lam = lam_ref[...]
    log_sig = jnp.minimum(lam, 0.0) - jnp.log1p(jnp.exp(-jnp.abs(lam)))
    log_a = RG_C * r * log_sig
    a = jnp.exp(log_a)
    th = jnp.tanh(log_a)
    b = jnp.sqrt(-2.0 * th / (1.0 - th)) * (ig * conv)

    ones8 = jnp.ones((SUBLANES, x.shape[1]), F32)
    zeros8 = jnp.zeros((SUBLANES, x.shape[1]), F32)
    s = 1
    while s < t_blk:
        if s < SUBLANES:
            a_sh = _shift_rows(a, s, ones8)
            b_sh = _shift_rows(b, s, zeros8)
        else:
            a_sh = jnp.concatenate([jnp.ones((s, x.shape[1]), F32), a[:t_blk - s]], axis=0)
            b_sh = jnp.concatenate([jnp.zeros((s, x.shape[1]), F32), b[:t_blk - s]], axis=0)
        b = a * b_sh + b
        a = a * a_sh
        s *= 2
    h = a * state_ref[SUBLANES - 1:SUBLANES, :] + b
    state_ref[...] = h[t_blk - SUBLANES:]

    o_ref[...] = (jax.nn.gelu(gate_ref[...], approximate=True) * h).astype(o_ref.dtype)


def _rglru(u, conv_w, conv_b, wa, ba, wx, bx, lam, t_blk):
    s_len = u.shape[0]
    c = RG_BLOCK
    return pl.pallas_call(
        _rglru_kernel,
        grid=(RG_HEADS, s_len // t_blk),
        in_specs=[pl.BlockSpec((t_blk, c), lambda h, t: (t, h)),
                  pl.BlockSpec((t_blk, c), lambda h, t: (t, RG_HEADS + h)),
                  pl.BlockSpec((CONV_WIDTH, c), lambda h, t: (0, h)),
                  pl.BlockSpec((1, c), lambda h, t: (0, h)),
                  pl.BlockSpec((1, c, c), lambda h, t: (h, 0, 0)),
                  pl.BlockSpec((1, 1, c), lambda h, t: (h, 0, 0)),
                  pl.BlockSpec((1, c, c), lambda h, t: (h, 0, 0)),
                  pl.BlockSpec((1, 1, c), lambda h, t: (h, 0, 0)),
                  pl.BlockSpec((1, c), lambda h, t: (0, h))],
        out_specs=pl.BlockSpec((t_blk, c), lambda h, t: (t, h)),
        out_shape=jax.ShapeDtypeStruct((s_len, D_RNN), BF16),
        scratch_shapes=[pltpu.VMEM((SUBLANES, c), F32), pltpu.VMEM((SUBLANES, c), F32)],
        compiler_params=_params(("parallel", "arbitrary")),
        name="rglru",
    )(u, u, conv_w, conv_b, wa, ba, wx, bx, lam)


def _q_proj_kernel(h_ref, wdq_ref, qn_ref, wa_ref, wb_ref, cos_ref, sin_ref, q_ref):
    cq = jnp.dot(h_ref[...].astype(BF16), wdq_ref[...], preferred_element_type=F32)
    cq = _rms_norm(cq, qn_ref[...]).astype(BF16)
    qa = jnp.dot(cq, wa_ref[...], preferred_element_type=F32)
    qb = jnp.dot(cq, wb_ref[...], preferred_element_type=F32)
    cosp = cos_ref[...]
    sinp = sin_ref[...]
    for hd in range(MLA_HEADS):
        nope = qa[:, hd * QK_PAD:hd * QK_PAD + LANES]
        rope = qa[:, hd * QK_PAD + LANES:(hd + 1) * QK_PAD] * cosp + qb[:, hd * LANES:(hd + 1) * LANES] * sinp
        q_ref[hd, :, 0:LANES] = (nope * ATTN_SCALE).astype(BF16)
        q_ref[hd, :, LANES:QK_PAD] = (rope * ATTN_SCALE).astype(BF16)


def _q_proj(h, wdq, qn, wa, wb, cosp, sinp, tm):
    s_len = h.shape[0]
    full = lambda i: (0, 0)
    return pl.pallas_call(
        _q_proj_kernel,
        grid=(s_len // tm,),
        in_specs=[pl.BlockSpec((tm, D_MODEL), lambda i: (i, 0)),
                  pl.BlockSpec(wdq.shape, full),
                  pl.BlockSpec(qn.shape, full),
                  pl.BlockSpec(wa.shape, full),
                  pl.BlockSpec(wb.shape, full),
                  pl.BlockSpec((tm, LANES), lambda i: (i, 0)),
                  pl.BlockSpec((tm, LANES), lambda i: (i, 0))],
        out_specs=pl.BlockSpec((MLA_HEADS, tm, QK_PAD), lambda i: (0, i, 0)),
        out_shape=jax.ShapeDtypeStruct((MLA_HEADS, s_len, QK_PAD), BF16),
        compiler_params=_params(("parallel",)),
        name="q_proj",
    )(h, wdq, qn, wa, wb, cosp, sinp)


def _kv_proj_kernel(h_ref, wd_ref, kn_ref, wu_ref, cos_ref, sin_ref, k_ref, v_ref):
    ckv = jnp.dot(h_ref[...].astype(BF16), wd_ref[...], preferred_element_type=F32)
    c = _rms_norm(ckv[:, :KV_LORA_RANK], kn_ref[...]).astype(BF16)
    k_pe = (ckv[:, KV_LORA_RANK:KV_LORA_RANK + LANES] * cos_ref[...]
            + ckv[:, KV_LORA_RANK + LANES:KV_LORA_RANK + 2 * LANES] * sin_ref[...]).astype(BF16)
    kv = jnp.dot(c, wu_ref[...], preferred_element_type=F32)
    w = QK_NOPE_DIM + V_HEAD_DIM
    for hd in range(MLA_HEADS):
        k_ref[hd, :, 0:LANES] = kv[:, hd * w:hd * w + QK_NOPE_DIM].astype(BF16)
        k_ref[hd, :, LANES:QK_PAD] = k_pe
        v_ref[hd] = kv[:, hd * w + QK_NOPE_DIM:(hd + 1) * w].astype(BF16)


def _kv_proj(h, wd, kn, wu, cosp, sinp, tm):
    s_len = h.shape[0]
    full = lambda i: (0, 0)
    return pl.pallas_call(
        _kv_proj_kernel,
        grid=(s_len // tm,),
        in_specs=[pl.BlockSpec((tm, D_MODEL), lambda i: (i, 0)),
                  pl.BlockSpec(wd.shape, full),
                  pl.BlockSpec(kn.shape, full),
                  pl.BlockSpec(wu.shape, full),
                  pl.BlockSpec((tm, LANES), lambda i: (i, 0)),
                  pl.BlockSpec((tm, LANES), lambda i: (i, 0))],
        out_specs=[pl.BlockSpec((MLA_HEADS, tm, QK_PAD), lambda i: (0, i, 0)),
                   pl.BlockSpec((MLA_HEADS, tm, V_HEAD_DIM), lambda i: (0, i, 0))],
        out_shape=[jax.ShapeDtypeStruct((MLA_HEADS, s_len, QK_PAD), BF16),
                   jax.ShapeDtypeStruct((MLA_HEADS, s_len, V_HEAD_DIM), BF16)],
        compiler_params=_params(("parallel",)),
        name="kv_proj",
    )(h, wd, kn, wu, cosp, sinp)


N_SUB = 4


def _attn_kernel(q_ref, k_ref, v_ref, o_ref, vaug_ref, m_ref, acc_ref):
    t = q_ref.shape[1] // N_SUB
    i = pl.program_id(1)

    @pl.when(i == 0)
    def _():
        vaug_ref[:, 0:V_HEAD_DIM] = v_ref[0]
        vaug_ref[:, V_HEAD_DIM:] = jnp.ones((vaug_ref.shape[0], LANES), BF16)

    m_ref[...] = jnp.full(m_ref.shape, MASK_VALUE, F32)
    acc_ref[...] = jnp.zeros_like(acc_ref)

    def load_chunk(j):
        start = pl.multiple_of(j * t, t)
        return k_ref[0, pl.ds(start, t), :], vaug_ref[pl.ds(start, t), :]

    def update(sub, kc, vc, masked):
        q = q_ref[0, sub * t:(sub + 1) * t, :]
        s = lax.dot_general(q, kc, NT_DIMS, preferred_element_type=F32)
        if masked:
            row = lax.broadcasted_iota(I32, s.shape, 0)
            col = lax.broadcasted_iota(I32, s.shape, 1)
            s = jnp.where(col <= row, s, MASK_VALUE)
        m_prev = m_ref[sub]
        m_new = jnp.maximum(m_prev, jnp.max(s, axis=1, keepdims=True))
        alpha = jnp.exp(m_prev - m_new)
        p = jnp.exp(s - jnp.concatenate([m_new] * (t // LANES), axis=1))
        pv = jnp.dot(p.astype(BF16), vc, preferred_element_type=F32)
        acc_ref[sub] = acc_ref[sub] * jnp.concatenate([alpha, alpha], axis=1) + pv
        m_ref[sub] = m_new

    def body(jj, carry):
        for c in range(N_SUB):
            kc, vc = load_chunk(jj * N_SUB + c)
            for sub in range(N_SUB):
                update(sub, kc, vc, False)
        return carry

    lax.fori_loop(0, i, body, 0)
    for c in range(N_SUB):
        kc, vc = load_chunk(i * N_SUB + c)
        for sub in range(c, N_SUB):
            update(sub, kc, vc, sub == c)
    for sub in range(N_SUB):
        acc = acc_ref[sub]
        o_ref[sub * t:(sub + 1) * t, :] = (acc[:, 0:V_HEAD_DIM] / acc[:, V_HEAD_DIM:]).astype(o_ref.dtype)


def _attention(q, k, v, tq):
    s_len = q.shape[1]
    t = tq // N_SUB
    return pl.pallas_call(
        _attn_kernel,
        grid=(MLA_HEADS, s_len // tq),
        in_specs=[pl.BlockSpec((1, tq, QK_PAD), lambda h, i: (h, i, 0)),
                  pl.BlockSpec((1, s_len, QK_PAD), lambda h, i: (h, 0, 0)),
                  pl.BlockSpec((1, s_len, V_HEAD_DIM), lambda h, i: (h, 0, 0))],
        out_specs=pl.BlockSpec((tq, V_HEAD_DIM), lambda h, i: (i, h)),
        out_shape=jax.ShapeDtypeStruct((s_len, MLA_HEADS * V_HEAD_DIM), BF16),
        scratch_shapes=[pltpu.VMEM((s_len, 2 * LANES), BF16),
                        pltpu.VMEM((N_SUB, t, LANES), F32),
                        pltpu.VMEM((N_SUB, t, 2 * LANES), F32)],
        compiler_params=_params(("parallel", "arbitrary")),
        name="attention",
    )(q, k, v)


def _route_kernel(x_ref, wh_ref, wl_ref, b_ref, idx_ref, gate_ref, rank_ref, cnt_ref, carry_ref):
    tb = x_ref.shape[0]

    @pl.when(pl.program_id(0) == 0)
    def _():
        carry_ref[...] = jnp.zeros_like(carry_ref)

    x = x_ref[...]
    xh = x.astype(BF16)
    xl = (x - xh.astype(F32)).astype(BF16)
    wh = wh_ref[...]
    logits = (lax.dot_general(wh, xh, NT_DIMS, preferred_element_type=F32)
              + lax.dot_general(wh, xl, NT_DIMS, preferred_element_type=F32)
              + lax.dot_general(wl_ref[...], xh, NT_DIMS, preferred_element_type=F32)
              + b_ref[...])
    erow = lax.broadcasted_iota(I32, logits.shape, 0).astype(F32)
    vals, sels = [], []
    rem = logits
    for _ in range(TOP_K):
        mx = jnp.max(rem, axis=0, keepdims=True)
        ix = jnp.min(jnp.where(rem == mx, erow, float(N_EXPERTS)), axis=0, keepdims=True)
        sel = erow == ix
        rem = jnp.where(sel, NEG_BIG, rem)
        vals.append(mx)
        sels.append(sel)
        idx_ref[len(vals) - 1:len(vals), :] = ix.astype(I32)
    es = [jnp.exp(v - vals[0]) for v in vals]
    den = es[0] + es[1] + es[2] + es[3]
    onehot = jnp.zeros(logits.shape, F32)
    for k in range(TOP_K):
        gate_ref[k:k + 1, :] = es[k] / den
        onehot = onehot + sels[k].astype(F32)
    before = lax.broadcasted_iota(I32, (tb, tb), 0) < lax.broadcasted_iota(I32, (tb, tb), 1)
    excl = jnp.dot(onehot.astype(BF16), before.astype(BF16), preferred_element_type=F32)
    excl = excl + carry_ref[:, 0:1]
    for k in range(TOP_K):
        rank_ref[k:k + 1, :] = jnp.sum(jnp.where(sels[k], excl, 0.0), axis=0, keepdims=True).astype(I32)
    carry_ref[...] = carry_ref[...] + jnp.sum(onehot, axis=1, keepdims=True)
    cnt_ref[...] = carry_ref[...]


def _route(h, wt_hi, wt_lo, bias, tb):
    s_len = h.shape[0]
    full = lambda i: (0, 0)
    tok = pl.BlockSpec((TOP_K, tb), lambda i: (0, i))
    return pl.pallas_call(
        _route_kernel,
        grid=(s_len // tb,),
        in_specs=[pl.BlockSpec((tb, D_MODEL), lambda i: (i, 0)),
                  pl.BlockSpec(wt_hi.shape, full),
                  pl.BlockSpec(wt_lo.shape, full),
                  pl.BlockSpec(bias.shape, full)],
        out_specs=[tok, tok, tok, pl.BlockSpec((N_EXPERTS, LANES), full)],
        out_shape=[jax.ShapeDtypeStruct((TOP_K, s_len), I32),
                   jax.ShapeDtypeStruct((TOP_K, s_len), F32),
                   jax.ShapeDtypeStruct((TOP_K, s_len), I32),
                   jax.ShapeDtypeStruct((N_EXPERTS, LANES), F32)],
        scratch_shapes=[pltpu.VMEM((N_EXPERTS, LANES), F32)],
        compiler_params=_params(("arbitrary",)),
        name="route",
    )(h, wt_hi, wt_lo, bias)


def _pack_rows(x):
    return x


def _unpack_rows(w, dtype):
    return w.astype(dtype)


def _dispatch_kernel(last_ref, pos_ref, x_ref, xs_ref, buf_ref, zero_ref, sem_ref, zsem_ref, *, rb):
    i = pl.program_id(0)
    n_steps = pl.num_programs(0)
    td = x_ref.shape[0]
    n_rows = TOP_K * td
    slot = i % 2

    def row_copy(sl, t, dst_row):
        return pltpu.make_async_copy(buf_ref.at[sl, pl.ds(t, 1)], xs_ref.at[pl.ds(dst_row, 1)],
                                     sem_ref.at[sl])

    def wait_rows(sl):
        for _ in range(n_rows):
            row_copy(sl, 0, 0).wait()

    @pl.when(i == 0)
    def _():
        zero_ref[...] = jnp.zeros_like(zero_ref)

        def zero_copy(e):
            return pltpu.make_async_copy(zero_ref, xs_ref.at[pl.ds(last_ref[e] * rb, rb)], zsem_ref)

        for e in range(N_EXPERTS):
            @pl.when(last_ref[e] >= 0)
            def _():
                zero_copy(e).start()
        for e in range(N_EXPERTS):
            @pl.when(last_ref[e] >= 0)
            def _():
                zero_copy(e).wait()

    @pl.when(i >= 2)
    def _():
        wait_rows(slot)

    buf_ref[slot] = _pack_rows(x_ref[...])
    for t in range(td):
        for k in range(TOP_K):
            row_copy(slot, t, pos_ref[0, 0, k * td + t]).start()

    @pl.when(i == n_steps - 1)
    def _():
        wait_rows(slot)

        @pl.when(n_steps >= 2)
        def _():
            wait_rows(1 - slot)


def _dispatch(h, pos_blocks, last_blk, n_rows_total, rb, td):
    s_len = h.shape[0]
    half = D_MODEL
    grid_spec = pltpu.PrefetchScalarGridSpec(
        num_scalar_prefetch=1,
        grid=(s_len // td,),
        in_specs=[pl.BlockSpec((1, 1, TOP_K * td), lambda i, last: (i, 0, 0), memory_space=pltpu.SMEM),
                  pl.BlockSpec((td, D_MODEL), lambda i, last: (i, 0))],
        out_specs=pl.BlockSpec(memory_space=pl.ANY),
        scratch_shapes=[pltpu.VMEM((2, td, half), F32),
                        pltpu.VMEM((rb, half), F32),
                        pltpu.SemaphoreType.DMA((2,)),
                        pltpu.SemaphoreType.DMA(())],
    )
    return pl.pallas_call(
        functools.partial(_dispatch_kernel, rb=rb),
        grid_spec=grid_spec,
        out_shape=jax.ShapeDtypeStruct((n_rows_total, half), F32),
        compiler_params=_params(("arbitrary",)),
        name="dispatch",
    )(last_blk, pos_blocks, h)


WEIGHT_CAST_ROWS = 256


def _cast_matmul(x, stage_ref, wbf_ref):
    acc = None
    for c in range(wbf_ref.shape[0] // WEIGHT_CAST_ROWS):
        rows = slice(c * WEIGHT_CAST_ROWS, (c + 1) * WEIGHT_CAST_ROWS)
        wc = stage_ref[rows, :].astype(BF16)
        wbf_ref[rows, :] = wc
        part = jnp.dot(x[:, rows], wc, preferred_element_type=F32)
        acc = part if acc is None else acc + part
    return acc


def _swiglu(hh):
    glu = jnp.minimum(hh[:, :D_EXPERT], SWIGLU_LIMIT)
    lin = jnp.clip(hh[:, D_EXPERT:], -SWIGLU_LIMIT, SWIGLU_LIMIT)
    return (glu * _sigmoid(SWIGLU_ALPHA * glu) * (lin + 1.0)).astype(BF16)


def _mlp_kernel(tab_ref, nused_ref, xs_ref, b1_ref, b2_ref, w1_hbm, w2_hbm, y_ref,
                stage1_ref, stage2_ref, w1b_ref, w2b_ref, sem_ref, *, layer):
    b = pl.program_id(0)

    def copy1(e):
        return pltpu.make_async_copy(w1_hbm.at[layer, e], stage1_ref, sem_ref.at[0])

    def copy2(e):
        return pltpu.make_async_copy(w2_hbm.at[layer, e], stage2_ref, sem_ref.at[1])

    @pl.when(b < nused_ref[0])
    def _():
        x = _unpack_rows(xs_ref[...], BF16)
        first = tab_ref[2, b] == 1

        @pl.when(first)
        def _():
            e = tab_ref[0, b]
            nxt = tab_ref[3, b]

            @pl.when(b == 0)
            def _():
                copy1(e).start()
                copy2(e).start()

            copy1(e).wait()
            act = _swiglu(_cast_matmul(x, stage1_ref, w1b_ref) + b1_ref[0, 0])

            @pl.when(nxt >= 0)
            def _():
                copy1(nxt).start()

            copy2(e).wait()
            y = _cast_matmul(act, stage2_ref, w2b_ref) + b2_ref[0, 0]

            @pl.when(nxt >= 0)
            def _():
                copy2(nxt).start()

            y_ref[...] = _pack_rows(y)

        @pl.when(jnp.logical_not(first))
        def _():
            act = _swiglu(jnp.dot(x, w1b_ref[...], preferred_element_type=F32) + b1_ref[0, 0])
            y_ref[...] = _pack_rows(jnp.dot(act, w2b_ref[...], preferred_element_type=F32) + b2_ref[0, 0])


def _experts(xs, block_tab, n_used, w1, b1, w2, b2, layer, rb):
    n_rows = xs.shape[0]
    half = D_MODEL
    rows = lambda b, tab, nu: (tab[1, b], 0)
    bsel = lambda b, tab, nu: (layer, tab[0, b], 0, 0)
    return pl.pallas_call(
        functools.partial(_mlp_kernel, layer=layer),
        grid_spec=pltpu.PrefetchScalarGridSpec(
            num_scalar_prefetch=2, grid=(n_rows // rb,),
            in_specs=[pl.BlockSpec((rb, half), rows),
                      pl.BlockSpec((1, 1, 1, 2 * D_EXPERT), bsel),
                      pl.BlockSpec((1, 1, 1, D_MODEL), bsel),
                      pl.BlockSpec(memory_space=pl.ANY),
                      pl.BlockSpec(memory_space=pl.ANY)],
            out_specs=pl.BlockSpec((rb, half), rows),
            scratch_shapes=[pltpu.VMEM((D_MODEL, 2 * D_EXPERT), F32),
                            pltpu.VMEM((D_EXPERT, D_MODEL), F32),
                            pltpu.VMEM((D_MODEL, 2 * D_EXPERT), BF16),
                            pltpu.VMEM((D_EXPERT, D_MODEL), BF16),
                            pltpu.SemaphoreType.DMA((2,))]),
        out_shape=jax.ShapeDtypeStruct((n_rows, half), F32),
        compiler_params=_params(("arbitrary",)),
        name="moe_mlp",
    )(block_tab, n_used, xs, b1, b2, w1, w2)


def _combine_kernel(pos_ref, posn_ref, h_ref, gate_ref, p_ref, g_ref, b_ref, wp_ref, wg_ref, yb_ref,
                    o_ref, rows_ref, sem_ref):
    i = pl.program_id(0)
    n_steps = pl.num_programs(0)
    tc = h_ref.shape[0]
    slot = i % 2

    def row_copy(sl, k, t, src_row):
        return pltpu.make_async_copy(yb_ref.at[pl.ds(src_row, 1)], rows_ref.at[sl, k, pl.ds(t, 1)],
                                     sem_ref.at[sl])

    def fetch(sl, p_ref_):
        for t in range(tc):
            for k in range(TOP_K):
                row_copy(sl, k, t, p_ref_[0, 0, k * tc + t]).start()

    @pl.when(i == 0)
    def _():
        fetch(slot, pos_ref)

    @pl.when(i + 1 < n_steps)
    def _():
        fetch(1 - slot, posn_ref)

    for _ in range(TOP_K * tc):
        row_copy(slot, 0, 0, 0).wait()

    gates = gate_ref[...]
    ffn = gates[:, 0:1] * _unpack_rows(rows_ref[slot, 0], F32)
    for k in range(1, TOP_K):
        ffn = ffn + gates[:, k:k + 1] * _unpack_rows(rows_ref[slot, k], F32)
    h2 = _layer_norm(DEEPNORM_ALPHA * h_ref[...] + ffn, g_ref[...], b_ref[...])
    emb = jnp.dot(p_ref[...].astype(BF16), wp_ref[...], preferred_element_type=F32)
    gate = _sigmoid(jnp.dot(h2.astype(BF16), wg_ref[...], preferred_element_type=F32))
    o_ref[...] = h2 + emb * gate


def _combine(h, yb, pos_blocks, gates_t, p, g, b, wp, wg, tc):
    s_len = h.shape[0]
    n_steps = s_len // tc
    full = lambda i: (0, 0)
    smem_blk = (1, 1, TOP_K * tc)
    return pl.pallas_call(
        _combine_kernel,
        grid=(n_steps,),
        in_specs=[pl.BlockSpec(smem_blk, lambda i: (i, 0, 0), memory_space=pltpu.SMEM),
                  pl.BlockSpec(smem_blk, lambda i: (jnp.minimum(i + 1, n_steps - 1), 0, 0),
                               memory_space=pltpu.SMEM),
                  pl.BlockSpec((tc, D_MODEL), lambda i: (i, 0)),
                  pl.BlockSpec((tc, TOP_K), lambda i: (i, 0)),
                  pl.BlockSpec((tc, PLE_DIM), lambda i: (i, 0)),
                  pl.BlockSpec((1, D_MODEL), full),
                  pl.BlockSpec((1, D_MODEL), full),
                  pl.BlockSpec(wp.shape, full),
                  pl.BlockSpec(wg.shape, full),
                  pl.BlockSpec(memory_space=pl.ANY)],
        out_specs=pl.BlockSpec((tc, D_MODEL), lambda i: (i, 0)),
        out_shape=jax.ShapeDtypeStruct((s_len, D_MODEL), F32),
        scratch_shapes=[pltpu.VMEM((2, TOP_K, tc, D_MODEL), F32),
                        pltpu.SemaphoreType.DMA((2,))],
        compiler_params=_params(("arbitrary",)),
        name="moe_combine",
    )(pos_blocks, pos_blocks, h, gates_t, p, g, b, wp, wg, yb)


def _token_blocks(a, tb):
    s_len = a.shape[1]
    return a.reshape(TOP_K, s_len // tb, tb).transpose(1, 0, 2).reshape(s_len // tb, 1, TOP_K * tb)


ROW_BLOCK = 256
DISPATCH_TOKENS = 128
COMBINE_TOKENS = 256
ROUTE_TOKENS = 512


def _lookup(table, keys):
    hit = keys[..., None] == jnp.arange(table.shape[0], dtype=I32)
    return jnp.sum(jnp.where(hit, table, 0), axis=-1)


def _moe_ln_ple(h, p, router_w, router_b, w1, b1, w2, b2, layer, ln_g, ln_b, wp, wg):
    s_len = h.shape[0]
    rb = ROW_BLOCK
    wt = router_w.T
    wt_hi = wt.astype(BF16)
    wt_lo = (wt - wt_hi.astype(F32)).astype(BF16)
    idx, gates, rank, cnt = _route(h, wt_hi, wt_lo, router_b.reshape(N_EXPERTS, 1),
                                   min(ROUTE_TOKENS, s_len))

    experts = jnp.arange(N_EXPERTS, dtype=I32)
    counts = cnt[:, 0].astype(I32)
    nblk = (counts + rb - 1) // rb
    blk_end = jnp.cumsum(nblk)
    blk_start = blk_end - nblk
    n_blocks = (TOP_K * s_len + rb - 1) // rb + N_EXPERTS
    n_used = blk_end[-1]
    blk_ids = jnp.minimum(jnp.arange(n_blocks, dtype=I32), n_used - 1)
    block_e = jnp.minimum(jnp.sum((blk_end[None, :] <= blk_ids[:, None]).astype(I32), axis=1), N_EXPERTS - 1)
    first = _lookup(blk_start, block_e) == blk_ids
    later = jnp.logical_and(experts[None, :] > experts[:, None], nblk[None, :] > 0)
    next_e = jnp.min(jnp.where(later, experts[None, :], N_EXPERTS), axis=1)
    next_e = jnp.where(next_e < N_EXPERTS, next_e, -1)
    block_tab = jnp.stack([block_e, blk_ids, first.astype(I32), _lookup(next_e, block_e)]).astype(I32)
    last_blk = jnp.where(nblk > 0, blk_end - 1, -1).astype(I32)
    pos = _lookup(blk_start * rb, idx) + rank

    td = min(DISPATCH_TOKENS, s_len)
    xs = _dispatch(h, _token_blocks(pos, td), last_blk, n_blocks * rb, rb, td)
    yb = _experts(xs, block_tab, n_used.reshape(1).astype(I32), w1, b1, w2, b2, layer, rb)
    tc = min(COMBINE_TOKENS, s_len)
    return _combine(h, yb, _token_blocks(pos, tc), gates.T, p, ln_g, ln_b, wp, wg, tc)


def _rope_tables(positions):
    inv = 1.0 / (ROPE_THETA ** (jnp.arange(0, QK_ROPE_DIM, 2, dtype=F32) / QK_ROPE_DIM))
    ang = positions.astype(F32)[:, None] * inv
    cos, sin = jnp.cos(ang), jnp.sin(ang)
    zeros = jnp.zeros((positions.shape[0], LANES - QK_ROPE_DIM), F32)
    return (jnp.concatenate([cos, cos, zeros], axis=1), jnp.concatenate([-sin, sin, zeros], axis=1))


def _swap_halves(w):
    half = QK_ROPE_DIM // 2
    return jnp.concatenate([w[..., half:], w[..., :half]], axis=-1)


def kernel(x, p, positions, ln_mix_g, ln_mix_b, ln_ffn_g, ln_ffn_b, rg_w_in, rg_conv_w, rg_conv_b, rg_gate_a_w, rg_gate_a_b, rg_gate_x_w, rg_gate_x_b, rg_lambda, rg_w_out, mla_w_dq, mla_q_norm, mla_w_uq, mla_w_o, kv_w_dkv, kv_norm, kv_w_ukv, moe_router_w, moe_router_b, moe_w1, moe_b1, moe_w2, moe_b2, ple_w_proj, ple_w_gate):
    bsz, s_len, _ = x.shape
    assert bsz == 1
    h = x.reshape(s_len, D_MODEL)
    cosp, sinp = _rope_tables(positions.reshape(s_len))
    tm = min(512, s_len)
    k_heads = v_heads = None
    for i in range(DEPTH):
        if i < N_A_LAYERS:
            u = _linear(h, rg_w_in[i].astype(BF16), F32, min(1024, s_len), 1024)
            yg = _rglru(u, rg_conv_w[i], rg_conv_b[i].reshape(1, D_RNN),
                        rg_gate_a_w[i].astype(BF16), rg_gate_a_b[i].reshape(RG_HEADS, 1, RG_BLOCK),
                        rg_gate_x_w[i].astype(BF16), rg_gate_x_b[i].reshape(RG_HEADS, 1, RG_BLOCK),
                        rg_lambda[i].reshape(1, D_RNN), min(256, s_len))
            w_o = rg_w_out[i]
        else:
            j = i - N_A_LAYERS
            if k_heads is None:
                lat, rope = kv_w_dkv[:, :KV_LORA_RANK], kv_w_dkv[:, KV_LORA_RANK:]
                zc = jnp.zeros((D_MODEL, LANES - QK_ROPE_DIM), F32)
                wd = jnp.concatenate([lat, rope, zc, _swap_halves(rope), zc], axis=1).astype(BF16)
                k_heads, v_heads = _kv_proj(h, wd, kv_norm.reshape(1, -1), kv_w_ukv.astype(BF16),
                                            cosp, sinp, min(256, s_len))
            wq = mla_w_uq[j].reshape(Q_LORA_RANK, MLA_HEADS, QK_NOPE_DIM + QK_ROPE_DIM)
            nope, rope = wq[..., :QK_NOPE_DIM], wq[..., QK_NOPE_DIM:]
            zq = jnp.zeros((Q_LORA_RANK, MLA_HEADS, LANES - QK_ROPE_DIM), F32)
            wa = jnp.concatenate([nope, rope, zq], axis=-1).reshape(Q_LORA_RANK, MLA_HEADS * QK_PAD)
            wb = jnp.concatenate([_swap_halves(rope), zq], axis=-1).reshape(Q_LORA_RANK, MLA_HEADS * LANES)
            q_heads = _q_proj(h, mla_w_dq[j].astype(BF16), mla_q_norm[j].reshape(1, -1),
                              wa.astype(BF16), wb.astype(BF16), cosp, sinp, min(256, s_len))
            yg = _attention(q_heads, k_heads, v_heads, min(2048, s_len))
            w_o = mla_w_o[j]
        h = _proj_ln(yg, w_o.astype(BF16), h, ln_mix_g[i].reshape(1, -1), ln_mix_b[i].reshape(1, -1), tm)
        h = _moe_ln_ple(h, p[i].reshape(s_len, PLE_DIM), moe_router_w[i], moe_router_b[i],
                        moe_w1, moe_b1.reshape(DEPTH, N_EXPERTS, 1, -1),
                        moe_w2, moe_b2.reshape(DEPTH, N_EXPERTS, 1, -1), i,
                        ln_ffn_g[i].reshape(1, -1), ln_ffn_b[i].reshape(1, -1),
                        ple_w_proj[i].astype(BF16), ple_w_gate[i].astype(BF16))
    return h.reshape(bsz, s_len, D_MODEL)
```
